```python
import math
import jax, jax.numpy as jnp
from jax import lax
import numpy as np

D_MODEL = 1024
BATCH = 8
SEQ = 4096
DEPTH = 1

N_META = 16
BLOCK = 128
MLA_HEADS = 8
QK_NOPE = 64
QK_ROPE = 32
V_HEAD = 64
Q_LORA = 256
KV_LORA = 128
ROPE_THETA = 10000.0
MLA_WIDTH = MLA_HEADS * V_HEAD
CONV_CH = 512
CONV_K = 31
CONV_WIDTH = CONV_CH
MIX_WIDTH = MLA_WIDTH + CONV_WIDTH
IN_COLS = 2 * CONV_CH + Q_LORA + KV_LORA + QK_ROPE
N_KEYS = 128
N_EXPERTS = N_KEYS * N_KEYS
PEER_HEADS = 8
PEER_DK = 128
PEER_DK_HALF = PEER_DK // 2
PEER_TOPK = 16
PEER_CHUNK = 128
NORM_EPS = 1e-6

kernel_name = "hymba_conformer_mla_peer_block"


def rmsnorm(x, g):
    xf = x.astype(jnp.float32)
    y = xf * lax.rsqrt(jnp.mean(xf * xf, axis=-1, keepdims=True) + NORM_EPS)
    return (y * g.astype(jnp.float32)).astype(x.dtype)


def layernorm(x, g, b):
    xf = x.astype(jnp.float32)
    mu = jnp.mean(xf, axis=-1, keepdims=True)
    var = jnp.mean(jnp.square(xf - mu), axis=-1, keepdims=True)
    y = (xf - mu) * lax.rsqrt(var + NORM_EPS)
    return (y * g.astype(jnp.float32) + b.astype(jnp.float32)).astype(x.dtype)


def rope(x, pos):
    half = x.shape[-1] // 2
    freqs = ROPE_THETA ** (-jnp.arange(half, dtype=jnp.float32) / half)
    ang = pos.astype(jnp.float32)[:, None] * freqs[None, :]
    cos = jnp.cos(ang)[None, :, None, :].astype(x.dtype)
    sin = jnp.sin(ang)[None, :, None, :].astype(x.dtype)
    x1, x2 = x[..., :half], x[..., half:]
    return jnp.concatenate([x1 * cos - x2 * sin, x2 * cos + x1 * sin], axis=-1)


def causal_block_attention(q, k, v):
    B, Tp, H, Dq = q.shape
    nb = Tp // BLOCK
    qb = q.reshape(B, nb, BLOCK, H, Dq).transpose(1, 0, 2, 3, 4)
    kpos = jnp.arange(Tp)
    scale = 1.0 / math.sqrt(Dq)

    def one(args):
        qblk, i = args
        s = jnp.einsum('bqhd,bkhd->bhqk', qblk, k).astype(jnp.float32) * scale
        qpos = i * BLOCK + jnp.arange(BLOCK)
        mask = kpos[None, :] <= qpos[:, None]
        s = jnp.where(mask[None, None], s, -jnp.inf)
        p = jax.nn.softmax(s, axis=-1).astype(v.dtype)
        return jnp.einsum('bhqk,bkhd->bqhd', p, v)

    o = lax.map(one, (qb, jnp.arange(nb)))
    return o.transpose(1, 0, 2, 3, 4).reshape(B, Tp, H, v.shape[-1])


def mla_group(c_q, c_kv, k_rope_in, pos, g_q, w_uq, g_kv, w_ukv):
    B, Tp, _ = c_q.shape
    q = (rmsnorm(c_q, g_q) @ w_uq).reshape(B, Tp, MLA_HEADS, QK_NOPE + QK_ROPE)
    q = jnp.concatenate([q[..., :QK_NOPE], rope(q[..., QK_NOPE:], pos)], axis=-1)
    kv = (rmsnorm(c_kv, g_kv) @ w_ukv).reshape(B, Tp, MLA_HEADS, QK_NOPE + V_HEAD)
    k_nope, v = kv[..., :QK_NOPE], kv[..., QK_NOPE:]
    k_r = rope(k_rope_in[:, :, None, :], pos)
    k = jnp.concatenate([k_nope, jnp.broadcast_to(k_r, (B, Tp, MLA_HEADS, QK_ROPE))], axis=-1)
    o = causal_block_attention(q, k, v)
    return o.reshape(B, Tp, MLA_WIDTH)


def conv_group(a, b, conv_w, conv_b, g_ln, b_ln):
    h = a * jax.nn.sigmoid(b)
    h = lax.conv_general_dilated(h, conv_w[:, None, :].astype(h.dtype), window_strides=(1,),
                                 padding=[(CONV_K - 1, 0)],
                                 dimension_numbers=('NWC', 'WIO', 'NWC'),
                                 feature_group_count=CONV_CH) + conv_b
    h = layernorm(h, g_ln, b_ln)
    return jax.nn.silu(h)


def peer_ffn(xn, wq, keys, u_tab, v_tab):
    B, Tp, D = xn.shape
    xc = xn.reshape(-1, PEER_CHUNK, D)
    K = PEER_TOPK

    def one(xb):
        C = xb.shape[0]
        q = (xb @ wq).reshape(C, PEER_HEADS, 2, PEER_DK_HALF)
        s = jnp.einsum('chpd,hpnd->chpn', q, keys).astype(jnp.float32)
        sv, si = lax.top_k(s, K)
        cand = sv[:, :, 0, :, None] + sv[:, :, 1, None, :]
        cidx = si[:, :, 0, :, None] * N_KEYS + si[:, :, 1, None, :]
        top_s, sel = lax.top_k(cand.reshape(C, PEER_HEADS, K * K), K)
        eidx = jnp.take_along_axis(cidx.reshape(C, PEER_HEADS, K * K), sel, axis=-1)
        g = jax.nn.softmax(top_s, axis=-1)
        u = u_tab[eidx]
        act = jax.nn.gelu(jnp.einsum('chkd,cd->chk', u, xb).astype(jnp.float32), approximate=False)
        w = (g * act).astype(xb.dtype)
        return jnp.einsum('chk,chkd->cd', w, v_tab[eidx])

    return lax.map(one, xc).reshape(B, Tp, D)


def setup_inputs(seed: int = 0) -> dict:
    key = jax.random.key(seed)
    ks = jax.random.split(key, 24)
    f32 = jnp.float32
    L, D = DEPTH, D_MODEL

    def nrm(k, shape, scale):
        return jax.random.normal(k, shape, f32) * scale

    def gain(k, shape):
        return 1.0 + 0.02 * jax.random.normal(k, shape, f32)

    return {
        "x": jax.random.normal(ks[0], (BATCH, SEQ, D), f32),
        "meta": nrm(ks[1], (N_META, D), 1.0),
        "g_mix_norm": gain(ks[2], (L, D)),
        "w_in": nrm(ks[3], (L, D, IN_COLS), D ** -0.5),
        "g_q": gain(ks[4], (L, Q_LORA)),
        "w_uq": nrm(ks[5], (L, Q_LORA, MLA_HEADS * (QK_NOPE + QK_ROPE)), Q_LORA ** -0.5),
        "g_kv": gain(ks[6], (L, KV_LORA)),
        "w_ukv": nrm(ks[7], (L, KV_LORA, MLA_HEADS * (QK_NOPE + V_HEAD)), KV_LORA ** -0.5),
        "conv_w": nrm(ks[8], (L, CONV_K, CONV_CH), CONV_K ** -0.5),
        "conv_b": nrm(ks[9], (L, CONV_CH), 0.02),
        "g_conv_ln": gain(ks[10], (L, CONV_CH)),
        "b_conv_ln": nrm(ks[11], (L, CONV_CH), 0.02),
        "g_out_attn": gain(ks[12], (L, MLA_WIDTH)),
        "g_out_conv": gain(ks[13], (L, CONV_WIDTH)),
        "w_out": nrm(ks[14], (L, MIX_WIDTH, D), MIX_WIDTH ** -0.5),
        "g_ffn_norm": gain(ks[15], (L, D)),
        "peer_wq": nrm(ks[16], (L, D, PEER_HEADS * PEER_DK), D ** -0.5),
        "peer_keys": nrm(ks[17], (L, PEER_HEADS, 2, N_KEYS, PEER_DK_HALF), PEER_DK_HALF ** -0.5),
        "peer_u": nrm(ks[18], (L, N_EXPERTS, D), D ** -0.5),
        "peer_v": nrm(ks[19], (L, N_EXPERTS, D), (PEER_HEADS * PEER_TOPK) ** -0.5),
        "g_final": gain(ks[20], (D,)),
    }


def reference(x, meta, g_mix_norm, w_in, g_q, w_uq, g_kv, w_ukv, conv_w, conv_b, g_conv_ln, b_conv_ln,
              g_out_attn, g_out_conv, w_out, g_ffn_norm, peer_wq, peer_keys, peer_u, peer_v, g_final):
    B, S, D = x.shape
    T = N_META + S
    Tp = ((T + BLOCK - 1) // BLOCK) * BLOCK
    h = jnp.concatenate([jnp.broadcast_to(meta.astype(x.dtype)[None], (B, N_META, D)), x,
                         jnp.zeros((B, Tp - T, D), x.dtype)], axis=1)
    pos = jnp.arange(Tp, dtype=jnp.int32)

    c0 = 2 * CONV_CH
    c1 = c0 + Q_LORA
    c2 = c1 + KV_LORA
    for l in range(DEPTH):
        hn = rmsnorm(h, g_mix_norm[l])
        z = hn @ w_in[l]
        conv_out = conv_group(z[..., :CONV_CH], z[..., CONV_CH:c0], conv_w[l], conv_b[l],
                              g_conv_ln[l], b_conv_ln[l])
        attn_out = mla_group(z[..., c0:c1], z[..., c1:c2], z[..., c2:], pos,
                             g_q[l], w_uq[l], g_kv[l], w_ukv[l])
        mixed = jnp.concatenate([rmsnorm(attn_out, g_out_attn[l]),
                                 rmsnorm(conv_out, g_out_conv[l])], axis=-1)
        h = h + mixed @ w_out[l]
        h = h + peer_ffn(rmsnorm(h, g_ffn_norm[l]), peer_wq[l], peer_keys[l], peer_u[l], peer_v[l])

    h = rmsnorm(h, g_final)
    return h[:, N_META:N_META + S, :]
```

```python
import math
from functools import partial

import jax
import jax.numpy as jnp
from jax import lax
from jax.experimental import pallas as pl
from jax.experimental.pallas import tpu as pltpu

N_META = 16
BLOCK = 128
MLA_HEADS = 8
QK_NOPE = 64
QK_ROPE = 32
V_HEAD = 64
Q_LORA = 256
KV_LORA = 128
ROPE_THETA = 10000.0
MLA_WIDTH = MLA_HEADS * V_HEAD
CONV_CH = 512
CONV_K = 31
N_KEYS = 128
PEER_HEADS = 8
PEER_DK = 128
PEER_DK_HALF = PEER_DK // 2
PEER_TOPK = 16
PEER_CHUNK = 128
NORM_EPS = 1e-6


def _rmsnorm(x, g):
    xf = x.astype(jnp.float32)
    y = xf * lax.rsqrt(jnp.mean(xf * xf, axis=-1, keepdims=True) + NORM_EPS)
    return (y * g.astype(jnp.float32)).astype(x.dtype)


def _layernorm(x, g, b):
    xf = x.astype(jnp.float32)
    mu = jnp.mean(xf, axis=-1, keepdims=True)
    var = jnp.mean(jnp.square(xf - mu), axis=-1, keepdims=True)
    y = (xf - mu) * lax.rsqrt(var + NORM_EPS)
    return (y * g.astype(jnp.float32) + b.astype(jnp.float32)).astype(x.dtype)


def _rope(x, pos):
    half = x.shape[-1] // 2
    freqs = ROPE_THETA ** (-jnp.arange(half, dtype=jnp.float32) / half)
    ang = pos.astype(jnp.float32)[:, None] * freqs[None, :]
    cos = jnp.cos(ang)[None, :, None, :].astype(x.dtype)
    sin = jnp.sin(ang)[None, :, None, :].astype(x.dtype)
    x1, x2 = x[..., :half], x[..., half:]
    return jnp.concatenate([x1 * cos - x2 * sin, x2 * cos + x1 * sin], axis=-1)


def _attn(q, k, v):
    B, Tp, H, Dq = q.shape
    nb = Tp // BLOCK
    qb = q.reshape(B, nb, BLOCK, H, Dq).transpose(1, 0, 2, 3, 4)
    kpos = jnp.arange(Tp)
    scale = 1.0 / math.sqrt(Dq)

    def one(args):
        qblk, i = args
        s = jnp.einsum('bqhd,bkhd->bhqk', qblk, k).astype(jnp.float32) * scale
        qpos = i * BLOCK + jnp.arange(BLOCK)
        mask = kpos[None, :] <= qpos[:, None]
        s = jnp.where(mask[None, None], s, -jnp.inf)
        p = jax.nn.softmax(s, axis=-1).astype(v.dtype)
        return jnp.einsum('bhqk,bkhd->bqhd', p, v)

    o = lax.map(one, (qb, jnp.arange(nb)))
    return o.transpose(1, 0, 2, 3, 4).reshape(B, Tp, H, v.shape[-1])


def _peer(xn, wq, keys, u_tab, v_tab):
    B, Tp, D = xn.shape
    xc = xn.reshape(-1, PEER_CHUNK, D)
    K = PEER_TOPK

    def one(xb):
        C = xb.shape[0]
        q = (xb @ wq).reshape(C, PEER_HEADS, 2, PEER_DK_HALF)
        s = jnp.einsum('chpd,hpnd->chpn', q, keys).astype(jnp.float32)
        sv, si = lax.top_k(s, K)
        cand = sv[:, :, 0, :, None] + sv[:, :, 1, None, :]
        cidx = si[:, :, 0, :, None] * N_KEYS + si[:, :, 1, None, :]
        top_s, sel = lax.top_k(cand.reshape(C, PEER_HEADS, K * K), K)
        eidx = jnp.take_along_axis(cidx.reshape(C, PEER_HEADS, K * K), sel, axis=-1)
        g = jax.nn.softmax(top_s, axis=-1)
        u = u_tab[eidx]
        act = jax.nn.gelu(jnp.einsum('chkd,cd->chk', u, xb).astype(jnp.float32), approximate=False)
        w = (g * act).astype(xb.dtype)
        return jnp.einsum('chk,chkd->cd', w, v_tab[eidx])

    return lax.map(one, xc).reshape(B, Tp, D)


def _final_norm_kernel(h_ref, g_ref, o_ref):
    x = h_ref[...]
    y = x * lax.rsqrt(jnp.mean(x * x, axis=-1, keepdims=True) + NORM_EPS)
    o_ref[...] = y * g_ref[...]


def _final_norm(h, g):
    n, d = h.shape
    tile = 512
    return pl.pallas_call(
        _final_norm_kernel,
        grid=(n // tile,),
        in_specs=[pl.BlockSpec((tile, d), lambda i: (i, 0)), pl.BlockSpec((1, d), lambda i: (0, 0))],
        out_specs=pl.BlockSpec((tile, d), lambda i: (i, 0)),
        out_shape=jax.ShapeDtypeStruct((n, d), jnp.float32),
    )(h, g.reshape(1, d))


def kernel(x, meta, g_mix_norm, w_in, g_q, w_uq, g_kv, w_ukv, conv_w, conv_b, g_conv_ln, b_conv_ln, g_out_attn, g_out_conv, w_out, g_ffn_norm, peer_wq, peer_keys, peer_u, peer_v, g_final):
    B, S, D = x.shape
    T = N_META + S
    Tp = ((T + BLOCK - 1) // BLOCK) * BLOCK
    h = jnp.concatenate([jnp.broadcast_to(meta[None], (B, N_META, D)), x,
                         jnp.zeros((B, Tp - T, D), x.dtype)], axis=1)
    pos = jnp.arange(Tp, dtype=jnp.int32)
    c0 = 2 * CONV_CH
    c1 = c0 + Q_LORA
    c2 = c1 + KV_LORA
    l = 0
    hn = _rmsnorm(h, g_mix_norm[l])
    z = hn @ w_in[l]
    a, b = z[..., :CONV_CH], z[..., CONV_CH:c0]
    hc = a * jax.nn.sigmoid(b)
    hc = lax.conv_general_dilated(hc, conv_w[l][:, None, :], window_strides=(1,), padding=[(CONV_K - 1, 0)],
                                  dimension_numbers=('NWC', 'WIO', 'NWC'), feature_group_count=CONV_CH) + conv_b[l]
    conv_out = jax.nn.silu(_layernorm(hc, g_conv_ln[l], b_conv_ln[l]))
    c_q, c_kv, k_rope_in = z[..., c0:c1], z[..., c1:c2], z[..., c2:]
    q = (_rmsnorm(c_q, g_q[l]) @ w_uq[l]).reshape(B, Tp, MLA_HEADS, QK_NOPE + QK_ROPE)
    q = jnp.concatenate([q[..., :QK_NOPE], _rope(q[..., QK_NOPE:], pos)], axis=-1)
    kv = (_rmsnorm(c_kv, g_kv[l]) @ w_ukv[l]).reshape(B, Tp, MLA_HEADS, QK_NOPE + V_HEAD)
    k_nope, v = kv[..., :QK_NOPE], kv[..., QK_NOPE:]
    k_r = _rope(k_rope_in[:, :, None, :], pos)
    k = jnp.concatenate([k_nope, jnp.broadcast_to(k_r, (B, Tp, MLA_HEADS, QK_ROPE))], axis=-1)
    attn_out = _attn(q, k, v).reshape(B, Tp, MLA_WIDTH)
    mixed = jnp.concatenate([_rmsnorm(attn_out, g_out_attn[l]), _rmsnorm(conv_out, g_out_conv[l])], axis=-1)
    h = h + mixed @ w_out[l]
    h = h + _peer(_rmsnorm(h, g_ffn_norm[l]), peer_wq[l], peer_keys[l], peer_u[l], peer_v[l])
    hs = h[:, N_META:N_META + S, :].reshape(B * S, D)
    return _final_norm(hs, g_final).reshape(B, S, D)
```

```python
import math
from functools import partial

import jax
import jax.numpy as jnp
from jax import lax
from jax.experimental import pallas as pl
from jax.experimental.pallas import tpu as pltpu

N_META = 16
MLA_HEADS = 8
QK_NOPE = 64
QK_ROPE = 32
V_HEAD = 64
Q_LORA = 256
KV_LORA = 128
ROPE_THETA = 10000.0
CONV_CH = 512
CONV_K = 31
N_KEYS = 128
PEER_HEADS = 8
PEER_DK_HALF = 64
PEER_TOPK = 16
NORM_EPS = 1e-6

LANES = 128
HEAD_PAD = LANES
FRONT = LANES
ZERO_ROWS = FRONT - N_META
ROW_TILE = 512
CONV_HIST = 32
TOPK_TILE = 256
N_SLOT = PEER_HEADS * PEER_TOPK
ROWS_PER_EXPERT = 4
PEER_GROUP = 32
VMEM_LIMIT = 48 * 1024 * 1024
TABLE_VMEM_LIMIT = 56 * 1024 * 1024

_NT = (((1,), (1,)), ((), ()))


def _rms(x):
    return x * lax.rsqrt(jnp.mean(x * x, axis=-1, keepdims=True) + NORM_EPS)


def _dot(a, b):
    return jnp.dot(a.astype(jnp.bfloat16), b, preferred_element_type=jnp.float32)


def _in_proj_kernel(h_ref, gmix_ref, win_ref, gq_ref, wq_ref, gkv_ref, wkv_ref, cos_ref, sin_ref,
                    glu_ref, q_ref, k_ref, v_ref):
    hn = _rms(h_ref[0]) * gmix_ref[...]
    z = _dot(hn, win_ref[...])
    c0 = 2 * CONV_CH
    c1 = c0 + Q_LORA
    c2 = c1 + KV_LORA
    a, b = z[:, :CONV_CH], z[:, CONV_CH:c0]
    glu_ref[0] = a * (1.0 / (1.0 + jnp.exp(-b)))
    cos, sin = cos_ref[...], sin_ref[...]
    qab = _dot(_rms(z[:, c0:c1]) * gq_ref[...], wq_ref[...])
    kvv = _dot(_rms(z[:, c1:c2]) * gkv_ref[...], wkv_ref[...])
    kr = z[:, c2:c2 + HEAD_PAD] * cos + z[:, c2 + HEAD_PAD:c2 + 2 * HEAD_PAD] * sin
    scale = 1.0 / math.sqrt(QK_NOPE + QK_ROPE)
    hw = MLA_HEADS * HEAD_PAD
    for h in range(MLA_HEADS):
        sl = slice(h * HEAD_PAD, (h + 1) * HEAD_PAD)
        sl2 = slice(hw + h * HEAD_PAD, hw + (h + 1) * HEAD_PAD)
        q_ref[0, h] = ((qab[:, sl] * cos + qab[:, sl2] * sin) * scale).astype(jnp.bfloat16)
        k_ref[0, h] = (kvv[:, sl] + kr).astype(jnp.bfloat16)
        v_ref[0, h] = kvv[:, sl2].astype(jnp.bfloat16)


def _in_proj(h, weights, cos, sin, tile):
    B, rows, D = h.shape
    full = lambda w: pl.BlockSpec(w.shape, lambda b, i: (0,) * w.ndim)
    qkv_shape = jax.ShapeDtypeStruct((B, MLA_HEADS, rows, HEAD_PAD), jnp.bfloat16)
    qkv_spec = pl.BlockSpec((1, MLA_HEADS, tile, HEAD_PAD), lambda b, i: (b, 0, i, 0))
    return pl.pallas_call(
        _in_proj_kernel,
        grid=(B, rows // tile),
        in_specs=[pl.BlockSpec((1, tile, D), lambda b, i: (b, i, 0))] + [full(w) for w in weights] + [
            pl.BlockSpec((tile, HEAD_PAD), lambda b, i: (i, 0)),
            pl.BlockSpec((tile, HEAD_PAD), lambda b, i: (i, 0)),
        ],
        out_specs=[pl.BlockSpec((1, tile, CONV_CH), lambda b, i: (b, i, 0)), qkv_spec, qkv_spec, qkv_spec],
        out_shape=[jax.ShapeDtypeStruct((B, rows, CONV_CH), jnp.float32), qkv_shape, qkv_shape, qkv_shape],
        compiler_params=pltpu.CompilerParams(vmem_limit_bytes=VMEM_LIMIT),
        name="in_proj",
    )(h, *weights, cos, sin)


CONV_CHUNK = 64


def _conv_kernel(mhist_ref, hist_ref, cur_ref, w_ref, b_ref, gln_ref, bln_ref, gout_ref, o_ref, buf_ref):
    buf_ref[0:CONV_HIST] = jnp.where(pl.program_id(1) == 0, mhist_ref[0], hist_ref[0])
    buf_ref[CONV_HIST:] = cur_ref[0]
    base = CONV_HIST - (CONV_K - 1)
    for c in range(ROW_TILE // CONV_CHUNK):
        acc = jnp.zeros((CONV_CHUNK, CONV_CH), jnp.float32)
        for k in range(CONV_K):
            acc = acc + buf_ref[pl.ds(c * CONV_CHUNK + base + k, CONV_CHUNK), :] * w_ref[k:k + 1, :]
        y = acc + b_ref[...]
        mu = jnp.mean(y, axis=-1, keepdims=True)
        yc = y - mu
        var = jnp.mean(yc * yc, axis=-1, keepdims=True)
        y = yc * lax.rsqrt(var + NORM_EPS) * gln_ref[...] + bln_ref[...]
        y = y * (1.0 / (1.0 + jnp.exp(-y)))
        o_ref[0, pl.ds(c * CONV_CHUNK, CONV_CHUNK), :] = (_rms(y) * gout_ref[...]).astype(jnp.bfloat16)


def _conv_branch(glu_meta, glu, w, b, gln, bln, gout):
    B, S, _ = glu.shape
    full = lambda shape: pl.BlockSpec(shape, lambda b, j: (0,) * len(shape))
    per = ROW_TILE // CONV_HIST
    return pl.pallas_call(
        _conv_kernel,
        grid=(B, S // ROW_TILE),
        in_specs=[
            pl.BlockSpec((1, CONV_HIST, CONV_CH), lambda b, j: (0, FRONT // CONV_HIST - 1, 0)),
            pl.BlockSpec((1, CONV_HIST, CONV_CH), lambda b, j: (b, jnp.maximum(j * per - 1, 0), 0)),
            pl.BlockSpec((1, ROW_TILE, CONV_CH), lambda b, j: (b, j, 0)),
            full(w.shape), full(b.shape), full(gln.shape), full(bln.shape), full(gout.shape),
        ],
        out_specs=pl.BlockSpec((1, ROW_TILE, CONV_CH), lambda b, j: (b, j, 0)),
        out_shape=jax.ShapeDtypeStruct((B, S, CONV_CH), jnp.bfloat16),
        scratch_shapes=[pltpu.VMEM((CONV_HIST + ROW_TILE, CONV_CH), jnp.float32)],
        compiler_params=pltpu.CompilerParams(vmem_limit_bytes=VMEM_LIMIT),
        name="conv_branch",
    )(glu_meta, glu, glu, w, b, gln, bln, gout)


def _attn_kernel(q_ref, km_ref, vm_ref, k_ref, v_ref, o_ref):
    qi = pl.program_id(2)
    out = jnp.zeros((ROW_TILE, HEAD_PAD), jnp.float32)
    for hh in range(2):
        q = q_ref[0, hh]

        def block(kb, vb, mask, carry):
            m, l, acc = carry
            s = lax.dot_general(q, kb, _NT, preferred_element_type=jnp.float32)
            if mask is not None:
                s = jnp.where(mask, s, -jnp.inf)
            m_new = jnp.maximum(m, jnp.max(s, axis=-1, keepdims=True))
            alpha = jnp.exp(m - m_new)
            p = jnp.exp(s - m_new)
            l = alpha * l + jnp.sum(p, axis=-1, keepdims=True)
            acc = alpha * acc + jnp.dot(p.astype(jnp.bfloat16), vb, preferred_element_type=jnp.float32)
            return m_new, l, acc

        col = lax.broadcasted_iota(jnp.int32, (ROW_TILE, FRONT), 1)
        carry = (jnp.full((ROW_TILE, 1), -jnp.inf, jnp.float32), jnp.zeros((ROW_TILE, 1), jnp.float32),
                 jnp.zeros((ROW_TILE, HEAD_PAD), jnp.float32))
        carry = block(km_ref[0, hh], vm_ref[0, hh], col >= ZERO_ROWS, carry)

        def body(j, carry):
            start = pl.multiple_of(j * ROW_TILE, ROW_TILE)
            return block(k_ref[0, hh, pl.ds(start, ROW_TILE), :], v_ref[0, hh, pl.ds(start, ROW_TILE), :],
                         None, carry)

        carry = lax.fori_loop(0, qi, body, carry)
        start = pl.multiple_of(qi * ROW_TILE, ROW_TILE)
        row = lax.broadcasted_iota(jnp.int32, (ROW_TILE, ROW_TILE), 0)
        colr = lax.broadcasted_iota(jnp.int32, (ROW_TILE, ROW_TILE), 1)
        m, l, acc = block(k_ref[0, hh, pl.ds(start, ROW_TILE), :], v_ref[0, hh, pl.ds(start, ROW_TILE), :],
                          colr <= row, carry)
        out = out + acc / l
    o_ref[0] = out


def _attention(q, k_meta, v_meta, k, v):
    B, H, S, _ = q.shape
    kv_spec = pl.BlockSpec((1, 2, S, HEAD_PAD), lambda b, hp, i: (b, hp, 0, 0))
    meta_spec = pl.BlockSpec((1, 2, FRONT, HEAD_PAD), lambda b, hp, i: (0, hp, 0, 0))
    return pl.pallas_call(
        _attn_kernel,
        grid=(B, H // 2, S // ROW_TILE),
        in_specs=[pl.BlockSpec((1, 2, ROW_TILE, HEAD_PAD), lambda b, hp, i: (b, hp, i, 0)), meta_spec, meta_spec,
                  kv_spec, kv_spec],
        out_specs=pl.BlockSpec((1, ROW_TILE, HEAD_PAD), lambda b, hp, i: (b, i, hp)),
        out_shape=jax.ShapeDtypeStruct((B, S, H // 2 * HEAD_PAD), jnp.float32),
        compiler_params=pltpu.CompilerParams(vmem_limit_bytes=VMEM_LIMIT),
        name="attention",
    )(q, k_meta, v_meta, k, v)


def _out_proj_kernel(x_ref, attn_ref, conv_ref, gattn_ref, wa_ref, wc_ref, gffn_ref, h_ref, xn_ref):
    an = _rms(attn_ref[...]) * gattn_ref[...]
    h = x_ref[...] + _dot(an, wa_ref[...]) + jnp.dot(conv_ref[...], wc_ref[...], preferred_element_type=jnp.float32)
    h_ref[...] = h
    xn_ref[...] = _rms(h) * gffn_ref[...]


def _out_proj(x2, attn, convn, gattn, wa, wc, gffn):
    n, d = x2.shape
    full = lambda shape: pl.BlockSpec(shape, lambda i: (0,) * len(shape))
    row = lambda w: pl.BlockSpec((ROW_TILE, w), lambda i: (i, 0))
    return pl.pallas_call(
        _out_proj_kernel,
        grid=(n // ROW_TILE,),
        in_specs=[row(d), row(attn.shape[1]), row(convn.shape[1]), full(gattn.shape), full(wa.shape), full(wc.shape),
                  full(gffn.shape)],
        out_specs=[row(d), row(d)],
        out_shape=[jax.ShapeDtypeStruct((n, d), jnp.float32)] * 2,
        compiler_params=pltpu.CompilerParams(vmem_limit_bytes=VMEM_LIMIT),
        name="out_proj",
    )(x2, attn, convn, gattn, wa, wc, gffn)


def _fold_keys_kernel(keys_ref, wqt_ref, o_ref):
    o_ref[0] = jnp.dot(keys_ref[0], wqt_ref[...], preferred_element_type=jnp.float32,
                       precision=lax.Precision.HIGHEST)


def _fold_keys(keys, wq):
    d = wq.shape[0]
    n_grp = 2 * PEER_HEADS
    keys2 = keys.reshape(n_grp, N_KEYS, PEER_DK_HALF)
    src = lambda i: (i % PEER_HEADS) * 2 + i // PEER_HEADS
    folded = pl.pallas_call(
        _fold_keys_kernel,
        grid=(n_grp,),
        in_specs=[pl.BlockSpec((1, N_KEYS, PEER_DK_HALF), lambda i: (src(i), 0, 0)),
                  pl.BlockSpec((PEER_DK_HALF, d), lambda i: (src(i), 0))],
        out_specs=pl.BlockSpec((1, N_KEYS, d), lambda i: (i, 0, 0)),
        out_shape=jax.ShapeDtypeStruct((n_grp, N_KEYS, d), jnp.float32),
        name="fold_keys",
    )(keys2, wq.T)
    return folded.transpose(1, 0, 2).reshape(N_KEYS * n_grp, d).astype(jnp.bfloat16)


def _stair_pairs():
    return [(a, b) for a in range(PEER_TOPK) for b in range(PEER_TOPK) if (a + 1) * (b + 1) <= PEER_TOPK]


def _topk_kernel(xn_ref, wqk_ref, idx_ref, gate_ref, s_ref, sv_ref, si_ref, cv_ref, ci_ref, ts_ref, te_ref):
    T = TOPK_TILE
    n_grp = 2 * PEER_HEADS
    st = lax.dot_general(wqk_ref[...], xn_ref[...].astype(jnp.bfloat16), _NT, preferred_element_type=jnp.float32)
    s_ref[...] = st.reshape(N_KEYS, n_grp, T)
    neg = jnp.float32(-jnp.inf)

    def round1(k, carry):
        m = s_ref[0]
        for n in range(1, N_KEYS):
            m = jnp.maximum(m, s_ref[n])
        arg = jnp.full((n_grp, T), N_KEYS, jnp.int32)
        for n in range(N_KEYS - 1, -1, -1):
            arg = jnp.where(s_ref[n] == m, n, arg)
        for n in range(N_KEYS):
            s_ref[n] = jnp.where(arg == n, neg, s_ref[n])
        sv_ref[k] = m
        si_ref[k] = arg
        return carry

    lax.fori_loop(0, PEER_TOPK, round1, 0)

    pairs = _stair_pairs()
    for c, (a, b) in enumerate(pairs):
        cv_ref[c] = sv_ref[a, 0:PEER_HEADS, :] + sv_ref[b, PEER_HEADS:n_grp, :]
        ci_ref[c] = (si_ref[a, 0:PEER_HEADS, :] * N_KEYS + si_ref[b, PEER_HEADS:n_grp, :]) * ROWS_PER_EXPERT

    def round2(k, carry):
        m = cv_ref[0]
        for c in range(1, len(pairs)):
            m = jnp.maximum(m, cv_ref[c])
        pos = jnp.full((PEER_HEADS, T), len(pairs), jnp.int32)
        for c in range(len(pairs) - 1, -1, -1):
            pos = jnp.where(cv_ref[c] == m, c, pos)
        e = jnp.zeros((PEER_HEADS, T), jnp.int32)
        for c in range(len(pairs)):
            hit = pos == c
            e = jnp.where(hit, ci_ref[c], e)
            cv_ref[c] = jnp.where(hit, neg, cv_ref[c])
        ts_ref[k] = m
        te_ref[k] = e
        return carry

    lax.fori_loop(0, PEER_TOPK, round2, 0)

    ts = ts_ref[...]
    ex = jnp.exp(ts - ts[0:1])
    gate = ex / jnp.sum(ex, axis=0, keepdims=True)
    gate_ref[...] = gate.reshape(N_SLOT, T).T
    idx_ref[...] = te_ref[...].reshape(N_SLOT, T).T


def _peer_topk(xn, wqk):
    n, d = xn.shape
    T = TOPK_TILE
    n_grp = 2 * PEER_HEADS
    n_pairs = len(_stair_pairs())
    return pl.pallas_call(
        _topk_kernel,
        grid=(n // T,),
        in_specs=[pl.BlockSpec((T, d), lambda i: (i, 0)), pl.BlockSpec(wqk.shape, lambda i: (0, 0))],
        out_specs=[pl.BlockSpec((T, N_SLOT), lambda i: (i, 0))] * 2,
        out_shape=[jax.ShapeDtypeStruct((n, N_SLOT), jnp.int32), jax.ShapeDtypeStruct((n, N_SLOT), jnp.float32)],
        scratch_shapes=[
            pltpu.VMEM((N_KEYS, n_grp, T), jnp.float32),
            pltpu.VMEM((PEER_TOPK, n_grp, T), jnp.float32),
            pltpu.VMEM((PEER_TOPK, n_grp, T), jnp.int32),
            pltpu.VMEM((n_pairs, PEER_HEADS, T), jnp.float32),
            pltpu.VMEM((n_pairs, PEER_HEADS, T), jnp.int32),
            pltpu.VMEM((PEER_TOPK, PEER_HEADS, T), jnp.float32),
            pltpu.VMEM((PEER_TOPK, PEER_HEADS, T), jnp.int32),
        ],
        compiler_params=pltpu.CompilerParams(vmem_limit_bytes=VMEM_LIMIT),
        name="peer_topk",
    )(xn, wqk)


def _pack_table(t):
    e, d = t.shape
    bits = lax.bitcast_convert_type(t.astype(jnp.bfloat16), jnp.uint16).astype(jnp.uint32)
    word = (bits[:, d // 2:] << 16) | bits[:, :d // 2]
    return lax.bitcast_convert_type(word, jnp.int32).reshape(e * ROWS_PER_EXPERT, LANES)


def _unpack(word):
    lo = pltpu.bitcast(word << 16, jnp.float32)
    hi = pltpu.bitcast(word & jnp.int32(-65536), jnp.float32)
    return lo, hi


def _split_bf16(x):
    xh = x.astype(jnp.bfloat16)
    return xh, (x - xh.astype(jnp.float32)).astype(jnp.bfloat16)


def _gelu(x):
    return 0.5 * x * (1.0 + lax.erf(x * (1.0 / math.sqrt(2.0))))


def _expert_act_kernel(idx_ref, xf_ref, gate_ref, tab_ref, w_ref, p_ref):
    G = PEER_GROUP
    R = ROWS_PER_EXPERT

    def token(g, carry):
        xg = xf_ref[pl.ds(pl.multiple_of(g * 8, 8), 8), :]
        x_lo, x_hi = xg[0:R], xg[R:2 * R]
        base = pl.multiple_of(g * (N_SLOT * R), 8)
        for r in range(N_SLOT):
            row = pl.multiple_of(idx_ref[g, r], R)
            lo, hi = _unpack(tab_ref[pl.ds(row, R), :])
            p_ref[pl.ds(base + R * r, R), :] = lo * x_lo + hi * x_hi
        return carry

    lax.fori_loop(0, G, token, 0)
    q = p_ref[pl.ds(0, G * N_SLOT, stride=R), :]
    for k in range(1, R):
        q = q + p_ref[pl.ds(k, G * N_SLOT, stride=R), :]
    qh, ql = _split_bf16(q)
    ones = jnp.ones((8, LANES), jnp.bfloat16)
    s = (lax.dot_general(ones, qh, _NT, preferred_element_type=jnp.float32)
         + lax.dot_general(ones, ql, _NT, preferred_element_type=jnp.float32))
    for g in range(G):
        sg = s[0:1, g * N_SLOT:(g + 1) * N_SLOT]
        w_ref[g:g + 1, :] = gate_ref[g:g + 1, :] * _gelu(sg)


def _expert_act(idx, xf, gate, tab):
    n = idx.shape[0]
    G = PEER_GROUP
    return pl.pallas_call(
        _expert_act_kernel,
        grid=(n // G,),
        in_specs=[
            pl.BlockSpec((G, N_SLOT), lambda i: (i, 0), memory_space=pltpu.SMEM),
            pl.BlockSpec((G * 8, LANES), lambda i: (i, 0)),
            pl.BlockSpec((G, N_SLOT), lambda i: (i, 0)),
            pl.BlockSpec(tab.shape, lambda i: (0, 0), pipeline_mode=pl.Buffered(1)),
        ],
        out_specs=pl.BlockSpec((G, N_SLOT), lambda i: (i, 0)),
        out_shape=jax.ShapeDtypeStruct((n, N_SLOT), jnp.float32),
        scratch_shapes=[pltpu.VMEM((G * N_SLOT * ROWS_PER_EXPERT, LANES), jnp.float32)],
        compiler_params=pltpu.CompilerParams(vmem_limit_bytes=TABLE_VMEM_LIMIT),
        name="expert_act",
    )(idx, xf, gate, tab)


def _expert_mix_kernel(idx_ref, w_ref, tab_ref, o_ref):
    R = ROWS_PER_EXPERT

    def token(g, carry):
        acc_lo = jnp.zeros((R, LANES), jnp.float32)
        acc_hi = jnp.zeros((R, LANES), jnp.float32)
        for r in range(N_SLOT):
            row = pl.multiple_of(idx_ref[g, r], R)
            lo, hi = _unpack(tab_ref[pl.ds(row, R), :])
            w = w_ref[g, r]
            acc_lo = acc_lo + w * lo
            acc_hi = acc_hi + w * hi
        o_ref[pl.ds(pl.multiple_of(g * 8, 8), 8), :] = jnp.concatenate([acc_lo, acc_hi], axis=0)
        return carry

    lax.fori_loop(0, PEER_GROUP, token, 0)


def _expert_mix(idx, w, tab):
    n = idx.shape[0]
    G = PEER_GROUP
    smem = pl.BlockSpec((G, N_SLOT), lambda i: (i, 0), memory_space=pltpu.SMEM)
    return pl.pallas_call(
        _expert_mix_kernel,
        grid=(n // G,),
        in_specs=[smem, smem, pl.BlockSpec(tab.shape, lambda i: (0, 0), pipeline_mode=pl.Buffered(1))],
        out_specs=pl.BlockSpec((G * 8, LANES), lambda i: (i, 0)),
        out_shape=jax.ShapeDtypeStruct((n * 8, LANES), jnp.float32),
        compiler_params=pltpu.CompilerParams(vmem_limit_bytes=TABLE_VMEM_LIMIT),
        name="expert_mix",
    )(idx, w, tab)


def _final_kernel(h_ref, p_ref, g_ref, o_ref):
    o_ref[...] = _rms(h_ref[...] + p_ref[...]) * g_ref[...]


def _final_norm(h, p, g):
    n, d = h.shape
    row = pl.BlockSpec((ROW_TILE, d), lambda i: (i, 0))
    return pl.pallas_call(
        _final_kernel,
        grid=(n // ROW_TILE,),
        in_specs=[row, row, pl.BlockSpec((1, d), lambda i: (0, 0))],
        out_specs=row,
        out_shape=jax.ShapeDtypeStruct((n, d), jnp.float32),
        name="final_norm",
    )(h, p, g)


def _rope_tables(first_pos, rows):
    half = QK_ROPE // 2
    freqs = ROPE_THETA ** (-jnp.arange(half, dtype=jnp.float32) / half)
    pos = (jnp.arange(rows, dtype=jnp.int32) + first_pos).astype(jnp.float32)
    ang = pos[:, None] * freqs[None, :]
    one = jnp.ones((rows, QK_NOPE), jnp.float32)
    zero = jnp.zeros((rows, HEAD_PAD - QK_NOPE - QK_ROPE), jnp.float32)
    cos = jnp.concatenate([one, jnp.cos(ang), jnp.cos(ang), zero], axis=1)
    sin = jnp.concatenate([0 * one, jnp.sin(ang), jnp.sin(ang), zero], axis=1)
    return cos, sin


def _rope_cols(w):
    half = QK_ROPE // 2
    x1, x2 = w[..., :half], w[..., half:]
    z_front = jnp.zeros(w.shape[:-1] + (QK_NOPE,), w.dtype)
    z_back = jnp.zeros(w.shape[:-1] + (HEAD_PAD - QK_NOPE - QK_ROPE,), w.dtype)
    return (jnp.concatenate([z_front, x1, x2, z_back], axis=-1), jnp.concatenate([z_front, -x2, x1, z_back], axis=-1))


def _prep_weights(w_in, w_uq, w_ukv):
    d = w_in.shape[0]
    c2 = 2 * CONV_CH + Q_LORA + KV_LORA
    kr_a, kr_b = _rope_cols(w_in[:, c2:])
    win = jnp.concatenate([w_in[:, :c2], kr_a, kr_b], axis=1).astype(jnp.bfloat16)
    wq3 = w_uq.reshape(Q_LORA, MLA_HEADS, QK_NOPE + QK_ROPE)
    qa, qb = _rope_cols(wq3[..., QK_NOPE:])
    qa = qa.at[..., :QK_NOPE].set(wq3[..., :QK_NOPE])
    wq = jnp.concatenate([qa.reshape(Q_LORA, -1), qb.reshape(Q_LORA, -1)], axis=1).astype(jnp.bfloat16)
    wkv3 = w_ukv.reshape(KV_LORA, MLA_HEADS, QK_NOPE + V_HEAD)
    zk = jnp.zeros((KV_LORA, MLA_HEADS, HEAD_PAD - QK_NOPE), w_ukv.dtype)
    wk = jnp.concatenate([wkv3[..., :QK_NOPE], zk], axis=-1)
    zv = jnp.zeros((KV_LORA, MLA_HEADS, HEAD_PAD - V_HEAD), w_ukv.dtype)
    v_lo = jnp.concatenate([wkv3[..., QK_NOPE:], zv], axis=-1)
    v_hi = jnp.concatenate([zv, wkv3[..., QK_NOPE:]], axis=-1)
    odd = (jnp.arange(MLA_HEADS) % 2 == 1)[None, :, None]
    wv = jnp.where(odd, v_hi, v_lo)
    wkv = jnp.concatenate([wk.reshape(KV_LORA, -1), wv.reshape(KV_LORA, -1)], axis=1).astype(jnp.bfloat16)
    return win, wq, wkv


def kernel(x, meta, g_mix_norm, w_in, g_q, w_uq, g_kv, w_ukv, conv_w, conv_b, g_conv_ln, b_conv_ln, g_out_attn, g_out_conv, w_out, g_ffn_norm, peer_wq, peer_keys, peer_u, peer_v, g_final):
    B, S, D = x.shape
    assert S % ROW_TILE == 0
    row2 = lambda v: v.reshape(1, -1)
    win, wq, wkv = _prep_weights(w_in[0], w_uq[0], w_ukv[0])
    weights = (row2(g_mix_norm[0]), win, row2(g_q[0]), wq, row2(g_kv[0]), wkv)
    h_meta = jnp.concatenate([jnp.zeros((ZERO_ROWS, D), x.dtype), meta.astype(x.dtype)], axis=0)[None]
    glu_meta, _, k_meta, v_meta = _in_proj(h_meta, weights, *_rope_tables(-ZERO_ROWS, FRONT), FRONT)
    glu, q, k, v = _in_proj(x, weights, *_rope_tables(N_META, S), ROW_TILE)
    convn = _conv_branch(glu_meta, glu, conv_w[0], row2(conv_b[0]), row2(g_conv_ln[0]), row2(b_conv_ln[0]),
                         row2(g_out_conv[0]))
    attn = _attention(q, k_meta, v_meta, k, v)
    n = B * S
    wo = w_out[0].astype(jnp.bfloat16)
    aw = MLA_HEADS * V_HEAD
    h1, xn = _out_proj(x.reshape(n, D), attn.reshape(n, aw), convn.reshape(n, CONV_CH), row2(g_out_attn[0]),
                       wo[:aw], wo[aw:], row2(g_ffn_norm[0]))
    wqk = _fold_keys(peer_keys[0], peer_wq[0])
    idx, gate = _peer_topk(xn, wqk)
    w = _expert_act(idx, xn.reshape(n * 8, LANES), gate, _pack_table(peer_u[0]))
    peer = _expert_mix(idx, w, _pack_table(peer_v[0])).reshape(n, D)
    return _final_norm(h1, peer, row2(g_final)).reshape(B, S, D)
```

```python
import math
from functools import partial

import jax
import jax.numpy as jnp
from jax import lax
from jax.experimental import pallas as pl
from jax.experimental.pallas import tpu as pltpu

N_META = 16
MLA_HEADS = 8
QK_NOPE = 64
QK_ROPE = 32
V_HEAD = 64
Q_LORA = 256
KV_LORA = 128
ROPE_THETA = 10000.0
CONV_CH = 512
CONV_K = 31
N_KEYS = 128
PEER_HEADS = 8
PEER_DK_HALF = 64
PEER_TOPK = 16
NORM_EPS = 1e-6

LANES = 128
HEAD_PAD = LANES
FRONT = LANES
ZERO_ROWS = FRONT - N_META
ROW_TILE = 512
CONV_HIST = 32
TOPK_TILE = 512
N_SLOT = PEER_HEADS * PEER_TOPK
ROWS_PER_EXPERT = 4
FOLD = 2 * ROWS_PER_EXPERT
PEER_GROUP = 32
VMEM_LIMIT = 48 * 1024 * 1024
TABLE_VMEM_LIMIT = 56 * 1024 * 1024

_NT = (((1,), (1,)), ((), ()))


def _rms(x):
    return x * lax.rsqrt(jnp.mean(x * x, axis=-1, keepdims=True) + NORM_EPS)


def _dot(a, b):
    return jnp.dot(a.astype(jnp.bfloat16), b, preferred_element_type=jnp.float32)


def _in_proj_kernel(h_ref, gmix_ref, win_ref, gq_ref, wq_ref, gkv_ref, wkv_ref, cos_ref, sin_ref,
                    glu_ref, q_ref, k_ref, v_ref):
    hn = _rms(h_ref[0]) * gmix_ref[...]
    z = _dot(hn, win_ref[...])
    c0 = 2 * CONV_CH
    c1 = c0 + Q_LORA
    c2 = c1 + KV_LORA
    a, b = z[:, :CONV_CH], z[:, CONV_CH:c0]
    glu_ref[0] = a * (1.0 / (1.0 + jnp.exp(-b)))
    cos, sin = cos_ref[...], sin_ref[...]
    qab = _dot(_rms(z[:, c0:c1]) * gq_ref[...], wq_ref[...])
    kvv = _dot(_rms(z[:, c1:c2]) * gkv_ref[...], wkv_ref[...])
    kr = z[:, c2:c2 + HEAD_PAD] * cos + z[:, c2 + HEAD_PAD:c2 + 2 * HEAD_PAD] * sin
    scale = 1.0 / math.sqrt(QK_NOPE + QK_ROPE)
    hw = MLA_HEADS * HEAD_PAD
    for h in range(MLA_HEADS):
        sl = slice(h * HEAD_PAD, (h + 1) * HEAD_PAD)
        sl2 = slice(hw + h * HEAD_PAD, hw + (h + 1) * HEAD_PAD)
        q_ref[0, h] = ((qab[:, sl] * cos + qab[:, sl2] * sin) * scale).astype(jnp.bfloat16)
        k_ref[0, h] = (kvv[:, sl] + kr).astype(jnp.bfloat16)
        v_ref[0, h] = kvv[:, sl2].astype(jnp.bfloat16)


def _in_proj(h, weights, cos, sin, tile):
    B, rows, D = h.shape
    full = lambda w: pl.BlockSpec(w.shape, lambda b, i: (0,) * w.ndim)
    qkv_shape = jax.ShapeDtypeStruct((B, MLA_HEADS, rows, HEAD_PAD), jnp.bfloat16)
    qkv_spec = pl.BlockSpec((1, MLA_HEADS, tile, HEAD_PAD), lambda b, i: (b, 0, i, 0))
    return pl.pallas_call(
        _in_proj_kernel,
        grid=(B, rows // tile),
        in_specs=[pl.BlockSpec((1, tile, D), lambda b, i: (b, i, 0))] + [full(w) for w in weights] + [
            pl.BlockSpec((tile, HEAD_PAD), lambda b, i: (i, 0)),
            pl.BlockSpec((tile, HEAD_PAD), lambda b, i: (i, 0)),
        ],
        out_specs=[pl.BlockSpec((1, tile, CONV_CH), lambda b, i: (b, i, 0)), qkv_spec, qkv_spec, qkv_spec],
        out_shape=[jax.ShapeDtypeStruct((B, rows, CONV_CH), jnp.float32), qkv_shape, qkv_shape, qkv_shape],
        compiler_params=pltpu.CompilerParams(vmem_limit_bytes=VMEM_LIMIT),
        name="in_proj",
    )(h, *weights, cos, sin)


CONV_CHUNK = 64


def _conv_kernel(mhist_ref, hist_ref, cur_ref, w_ref, b_ref, gln_ref, bln_ref, gout_ref, o_ref, buf_ref):
    buf_ref[0:CONV_HIST] = jnp.where(pl.program_id(1) == 0, mhist_ref[0], hist_ref[0])
    buf_ref[CONV_HIST:] = cur_ref[0]
    base = CONV_HIST - (CONV_K - 1)
    for c in range(ROW_TILE // CONV_CHUNK):
        acc = jnp.zeros((CONV_CHUNK, CONV_CH), jnp.float32)
        for k in range(CONV_K):
            acc = acc + buf_ref[pl.ds(c * CONV_CHUNK + base + k, CONV_CHUNK), :] * w_ref[k:k + 1, :]
        y = acc + b_ref[...]
        mu = jnp.mean(y, axis=-1, keepdims=True)
        yc = y - mu
        var = jnp.mean(yc * yc, axis=-1, keepdims=True)
        y = yc * lax.rsqrt(var + NORM_EPS) * gln_ref[...] + bln_ref[...]
        y = y * (1.0 / (1.0 + jnp.exp(-y)))
        o_ref[0, pl.ds(c * CONV_CHUNK, CONV_CHUNK), :] = (_rms(y) * gout_ref[...]).astype(jnp.bfloat16)


def _conv_branch(glu_meta, glu, w, b, gln, bln, gout):
    B, S, _ = glu.shape
    full = lambda shape: pl.BlockSpec(shape, lambda b, j: (0,) * len(shape))
    per = ROW_TILE // CONV_HIST
    return pl.pallas_call(
        _conv_kernel,
        grid=(B, S // ROW_TILE),
        in_specs=[
            pl.BlockSpec((1, CONV_HIST, CONV_CH), lambda b, j: (0, FRONT // CONV_HIST - 1, 0)),
            pl.BlockSpec((1, CONV_HIST, CONV_CH), lambda b, j: (b, jnp.maximum(j * per - 1, 0), 0)),
            pl.BlockSpec((1, ROW_TILE, CONV_CH), lambda b, j: (b, j, 0)),
            full(w.shape), full(b.shape), full(gln.shape), full(bln.shape), full(gout.shape),
        ],
        out_specs=pl.BlockSpec((1, ROW_TILE, CONV_CH), lambda b, j: (b, j, 0)),
        out_shape=jax.ShapeDtypeStruct((B, S, CONV_CH), jnp.bfloat16),
        scratch_shapes=[pltpu.VMEM((CONV_HIST + ROW_TILE, CONV_CH), jnp.float32)],
        compiler_params=pltpu.CompilerParams(vmem_limit_bytes=VMEM_LIMIT),
        name="conv_branch",
    )(glu_meta, glu, glu, w, b, gln, bln, gout)


def _attn_kernel(q_ref, km_ref, vm_ref, k_ref, v_ref, o_ref):
    qi = pl.program_id(2)
    heads = range(2)
    qs = [q_ref[0, hh] for hh in heads]

    def block(q, kb, vb, mask, carry):
        m, l, acc = carry
        s = lax.dot_general(q, kb, _NT, preferred_element_type=jnp.float32)
        if mask is not None:
            s = jnp.where(mask, s, -jnp.inf)
        m_new = jnp.maximum(m, jnp.max(s, axis=-1, keepdims=True))
        alpha = jnp.exp(m - m_new)
        p = jnp.exp(s - m_new)
        l = alpha * l + jnp.sum(p, axis=-1, keepdims=True)
        acc = alpha * acc + jnp.dot(p.astype(jnp.bfloat16), vb, preferred_element_type=jnp.float32)
        return m_new, l, acc

    def both(kv, mask, carries):
        return tuple(block(qs[hh], kv(k_ref, hh), kv(v_ref, hh), mask, carries[hh]) for hh in heads)

    init = (jnp.full((ROW_TILE, 1), -jnp.inf, jnp.float32), jnp.zeros((ROW_TILE, 1), jnp.float32),
            jnp.zeros((ROW_TILE, HEAD_PAD), jnp.float32))
    col = lax.broadcasted_iota(jnp.int32, (ROW_TILE, FRONT), 1)
    carries = tuple(block(qs[hh], km_ref[0, hh], vm_ref[0, hh], col >= ZERO_ROWS, init) for hh in heads)

    def body(j, carries):
        start = pl.multiple_of(j * ROW_TILE, ROW_TILE)
        return both(lambda ref, hh: ref[0, hh, pl.ds(start, ROW_TILE), :], None, carries)

    carries = lax.fori_loop(0, qi, body, carries)
    start = pl.multiple_of(qi * ROW_TILE, ROW_TILE)
    row = lax.broadcasted_iota(jnp.int32, (ROW_TILE, ROW_TILE), 0)
    colr = lax.broadcasted_iota(jnp.int32, (ROW_TILE, ROW_TILE), 1)
    carries = both(lambda ref, hh: ref[0, hh, pl.ds(start, ROW_TILE), :], colr <= row, carries)
    o_ref[0] = carries[0][2] / carries[0][1] + carries[1][2] / carries[1][1]


def _attention(q, k_meta, v_meta, k, v):
    B, H, S, _ = q.shape
    kv_spec = pl.BlockSpec((1, 2, S, HEAD_PAD), lambda b, hp, i: (b, hp, 0, 0))
    meta_spec = pl.BlockSpec((1, 2, FRONT, HEAD_PAD), lambda b, hp, i: (0, hp, 0, 0))
    return pl.pallas_call(
        _attn_kernel,
        grid=(B, H // 2, S // ROW_TILE),
        in_specs=[pl.BlockSpec((1, 2, ROW_TILE, HEAD_PAD), lambda b, hp, i: (b, hp, i, 0)), meta_spec, meta_spec,
                  kv_spec, kv_spec],
        out_specs=pl.BlockSpec((1, ROW_TILE, HEAD_PAD), lambda b, hp, i: (b, i, hp)),
        out_shape=jax.ShapeDtypeStruct((B, S, H // 2 * HEAD_PAD), jnp.float32),
        compiler_params=pltpu.CompilerParams(vmem_limit_bytes=VMEM_LIMIT),
        name="attention",
    )(q, k_meta, v_meta, k, v)


def _out_proj_kernel(x_ref, attn_ref, conv_ref, gattn_ref, wa_ref, wc_ref, gffn_ref, h_ref, xn_ref, xf_ref):
    an = _rms(attn_ref[...]) * gattn_ref[...]
    h = x_ref[...] + _dot(an, wa_ref[...]) + jnp.dot(conv_ref[...], wc_ref[...], preferred_element_type=jnp.float32)
    h_ref[...] = h
    xn = _rms(h) * gffn_ref[...]
    xn_ref[...] = xn
    for q in range(FOLD):
        xf_ref[pl.ds(q, ROW_TILE, stride=FOLD), :] = xn[:, q * LANES:(q + 1) * LANES]


def _out_proj(x2, attn, convn, gattn, wa, wc, gffn):
    n, d = x2.shape
    full = lambda shape: pl.BlockSpec(shape, lambda i: (0,) * len(shape))
    row = lambda w: pl.BlockSpec((ROW_TILE, w), lambda i: (i, 0))
    return pl.pallas_call(
        _out_proj_kernel,
        grid=(n // ROW_TILE,),
        in_specs=[row(d), row(attn.shape[1]), row(convn.shape[1]), full(gattn.shape), full(wa.shape), full(wc.shape),
                  full(gffn.shape)],
        out_specs=[row(d), row(d), pl.BlockSpec((ROW_TILE * FOLD, LANES), lambda i: (i, 0))],
        out_shape=[jax.ShapeDtypeStruct((n, d), jnp.float32)] * 2 + [jax.ShapeDtypeStruct((n * FOLD, LANES), jnp.float32)],
        compiler_params=pltpu.CompilerParams(vmem_limit_bytes=VMEM_LIMIT),
        name="out_proj",
    )(x2, attn, convn, gattn, wa, wc, gffn)


def _fold_keys_kernel(keys_ref, wqt_ref, o_ref):
    o_ref[0] = jnp.dot(keys_ref[0], wqt_ref[...], preferred_element_type=jnp.float32,
                       precision=lax.Precision.HIGHEST)


def _fold_keys(keys, wq):
    d = wq.shape[0]
    n_grp = 2 * PEER_HEADS
    keys2 = keys.reshape(n_grp, N_KEYS, PEER_DK_HALF)
    src = lambda i: (i % PEER_HEADS) * 2 + i // PEER_HEADS
    folded = pl.pallas_call(
        _fold_keys_kernel,
        grid=(n_grp,),
        in_specs=[pl.BlockSpec((1, N_KEYS, PEER_DK_HALF), lambda i: (src(i), 0, 0)),
                  pl.BlockSpec((PEER_DK_HALF, d), lambda i: (src(i), 0))],
        out_specs=pl.BlockSpec((1, N_KEYS, d), lambda i: (i, 0, 0)),
        out_shape=jax.ShapeDtypeStruct((n_grp, N_KEYS, d), jnp.float32),
        name="fold_keys",
    )(keys2, wq.T)
    return folded.transpose(1, 0, 2).reshape(N_KEYS * n_grp, d).astype(jnp.bfloat16)


def _stair_pairs():
    return [(a, b) for a in range(PEER_TOPK) for b in range(PEER_TOPK) if (a + 1) * (b + 1) <= PEER_TOPK]


def _topk_kernel(xn_ref, wqk_ref, idx_ref, gate_ref, s_ref, sv_ref, si_ref, cv_ref, ci_ref, ts_ref, te_ref):
    T = TOPK_TILE
    n_grp = 2 * PEER_HEADS
    st = lax.dot_general(wqk_ref[...], xn_ref[...].astype(jnp.bfloat16), _NT, preferred_element_type=jnp.float32)
    s_ref[...] = st.reshape(N_KEYS, n_grp, T)
    neg = jnp.float32(-jnp.inf)

    def round1(k, carry):
        m = s_ref[0]
        for n in range(1, N_KEYS):
            m = jnp.maximum(m, s_ref[n])
        arg = jnp.full((n_grp, T), N_KEYS, jnp.int32)
        for n in range(N_KEYS - 1, -1, -1):
            arg = jnp.where(s_ref[n] == m, n, arg)
        for n in range(N_KEYS):
            s_ref[n] = jnp.where(arg == n, neg, s_ref[n])
        sv_ref[k] = m
        si_ref[k] = arg
        return carry

    lax.fori_loop(0, PEER_TOPK, round1, 0)

    pairs = _stair_pairs()
    for c, (a, b) in enumerate(pairs):
        cv_ref[c] = sv_ref[a, 0:PEER_HEADS, :] + sv_ref[b, PEER_HEADS:n_grp, :]
        ci_ref[c] = (si_ref[a, 0:PEER_HEADS, :] * N_KEYS + si_ref[b, PEER_HEADS:n_grp, :]) * ROWS_PER_EXPERT

    def round2(k, carry):
        m = cv_ref[0]
        for c in range(1, len(pairs)):
            m = jnp.maximum(m, cv_ref[c])
        pos = jnp.full((PEER_HEADS, T), len(pairs), jnp.int32)
        for c in range(len(pairs) - 1, -1, -1):
            pos = jnp.where(cv_ref[c] == m, c, pos)
        e = jnp.zeros((PEER_HEADS, T), jnp.int32)
        for c in range(len(pairs)):
            hit = pos == c
            e = jnp.where(hit, ci_ref[c], e)
            cv_ref[c] = jnp.where(hit, neg, cv_ref[c])
        ts_ref[k] = m
        te_ref[k] = e
        return carry

    lax.fori_loop(0, PEER_TOPK, round2, 0)

    ts = ts_ref[...]
    ex = jnp.exp(ts - ts[0:1])
    gate = ex / jnp.sum(ex, axis=0, keepdims=True)
    gate_ref[...] = gate.reshape(N_SLOT, T).T
    idx_ref[...] = te_ref[...].reshape(N_SLOT, T).T


def _peer_topk(xn, wqk):
    n, d = xn.shape
    T = TOPK_TILE
    n_grp = 2 * PEER_HEADS
    n_pairs = len(_stair_pairs())
    return pl.pallas_call(
        _topk_kernel,
        grid=(n // T,),
        in_specs=[pl.BlockSpec((T, d), lambda i: (i, 0)), pl.BlockSpec(wqk.shape, lambda i: (0, 0))],
        out_specs=[pl.BlockSpec((T, N_SLOT), lambda i: (i, 0))] * 2,
        out_shape=[jax.ShapeDtypeStruct((n, N_SLOT), jnp.int32), jax.ShapeDtypeStruct((n, N_SLOT), jnp.float32)],
        scratch_shapes=[
            pltpu.VMEM((N_KEYS, n_grp, T), jnp.float32),
            pltpu.VMEM((PEER_TOPK, n_grp, T), jnp.float32),
            pltpu.VMEM((PEER_TOPK, n_grp, T), jnp.int32),
            pltpu.VMEM((n_pairs, PEER_HEADS, T), jnp.float32),
            pltpu.VMEM((n_pairs, PEER_HEADS, T), jnp.int32),
            pltpu.VMEM((PEER_TOPK, PEER_HEADS, T), jnp.float32),
            pltpu.VMEM((PEER_TOPK, PEER_HEADS, T), jnp.int32),
        ],
        compiler_params=pltpu.CompilerParams(vmem_limit_bytes=VMEM_LIMIT),
        name="peer_topk",
    )(xn, wqk)


def _pack_table(t):
    e, d = t.shape
    bits = lax.bitcast_convert_type(t.astype(jnp.bfloat16), jnp.uint16).astype(jnp.uint32)
    word = (bits[:, d // 2:] << 16) | bits[:, :d // 2]
    return lax.bitcast_convert_type(word, jnp.int32).reshape(e * ROWS_PER_EXPERT, LANES)


def _unpack(word):
    lo = pltpu.bitcast(word << 16, jnp.float32)
    hi = pltpu.bitcast(word & jnp.int32(-65536), jnp.float32)
    return lo, hi


def _split_bf16(x):
    xh = x.astype(jnp.bfloat16)
    return xh, (x - xh.astype(jnp.float32)).astype(jnp.bfloat16)


def _gelu(x):
    return 0.5 * x * (1.0 + lax.erf(x * (1.0 / math.sqrt(2.0))))


def _index_blocks(idx):
    n = idx.shape[0]
    G = PEER_GROUP
    return idx.reshape(n // G, G, N_SLOT).transpose(0, 2, 1).reshape(n // G, 1, N_SLOT * G)


def _with_index_block(idx_hbm, idx_bufs, sem, body):
    i = pl.program_id(0)
    nb = pl.num_programs(0)

    def copy(block, slot):
        return pltpu.make_async_copy(idx_hbm.at[block, 0], idx_bufs[slot], sem.at[slot])

    @pl.when(i == 0)
    def _():
        copy(0, 0).start()

    for slot in range(2):
        @pl.when(i % 2 == slot)
        def _():
            copy(i, slot).wait()

            @pl.when(i + 1 < nb)
            def _():
                copy(i + 1, 1 - slot).start()

            body(idx_bufs[slot])


def _expert_act_kernel(idx_hbm, xf_ref, gate_ref, tab_ref, w_ref, p_ref, idx_a, idx_b, sem):
    _with_index_block(idx_hbm, (idx_a, idx_b), sem, partial(_expert_act_body, xf_ref, gate_ref, tab_ref, w_ref, p_ref))


def _expert_act_body(xf_ref, gate_ref, tab_ref, w_ref, p_ref, idx_ref):
    G = PEER_GROUP
    R = ROWS_PER_EXPERT

    def token(g, carry):
        xg = xf_ref[pl.ds(pl.multiple_of(g * 8, 8), 8), :]
        x_lo, x_hi = xg[0:R], xg[R:2 * R]
        base = pl.multiple_of(g * (N_SLOT * R), 8)
        for r in range(N_SLOT):
            row = pl.multiple_of(idx_ref.at[pl.ds(r * G, G)][g], R)
            lo, hi = _unpack(tab_ref[pl.ds(row, R), :])
            p_ref[pl.ds(base + R * r, R), :] = lo * x_lo + hi * x_hi
        return carry

    lax.fori_loop(0, G, token, 0)
    q = p_ref[pl.ds(0, G * N_SLOT, stride=R), :]
    for k in range(1, R):
        q = q + p_ref[pl.ds(k, G * N_SLOT, stride=R), :]
    qh, ql = _split_bf16(q)
    ones = jnp.ones((8, LANES), jnp.bfloat16)
    s = (lax.dot_general(ones, qh, _NT, preferred_element_type=jnp.float32)
         + lax.dot_general(ones, ql, _NT, preferred_element_type=jnp.float32))
    for g in range(G):
        sg = s[0:1, g * N_SLOT:(g + 1) * N_SLOT]
        w_ref[g:g + 1, :] = gate_ref[g:g + 1, :] * _gelu(sg)


def _expert_act(idx_blocks, xf, gate, tab):
    G = PEER_GROUP
    n = idx_blocks.shape[0] * G
    return pl.pallas_call(
        _expert_act_kernel,
        grid=(n // G,),
        in_specs=[
            pl.BlockSpec(memory_space=pl.ANY),
            pl.BlockSpec((G * 8, LANES), lambda i: (i, 0)),
            pl.BlockSpec((G, N_SLOT), lambda i: (i, 0)),
            pl.BlockSpec(tab.shape, lambda i: (0, 0), pipeline_mode=pl.Buffered(1)),
        ],
        out_specs=pl.BlockSpec((G, N_SLOT), lambda i: (i, 0)),
        out_shape=jax.ShapeDtypeStruct((n, N_SLOT), jnp.float32),
        scratch_shapes=[pltpu.VMEM((G * N_SLOT * ROWS_PER_EXPERT, LANES), jnp.float32),
                        pltpu.SMEM((N_SLOT * G,), jnp.int32), pltpu.SMEM((N_SLOT * G,), jnp.int32),
                        pltpu.SemaphoreType.DMA((2,))],
        compiler_params=pltpu.CompilerParams(vmem_limit_bytes=TABLE_VMEM_LIMIT),
        name="expert_act",
    )(idx_blocks, xf, gate, tab)


def _expert_mix_kernel(idx_ref, w_ref, tab_ref, o_ref):
    R = ROWS_PER_EXPERT

    def token(g, carry):
        acc_lo = jnp.zeros((R, LANES), jnp.float32)
        acc_hi = jnp.zeros((R, LANES), jnp.float32)
        for r in range(N_SLOT):
            row = pl.multiple_of(idx_ref[g, r], R)
            lo, hi = _unpack(tab_ref[pl.ds(row, R), :])
            w = w_ref[g, r]
            acc_lo = acc_lo + w * lo
            acc_hi = acc_hi + w * hi
        o_ref[pl.ds(pl.multiple_of(g * 8, 8), 8), :] = jnp.concatenate([acc_lo, acc_hi], axis=0)
        return carry

    lax.fori_loop(0, PEER_GROUP, token, 0)


def _expert_mix(idx, w, tab):
    n = idx.shape[0]
    G = PEER_GROUP
    smem = pl.BlockSpec((G, N_SLOT), lambda i: (i, 0), memory_space=pltpu.SMEM)
    return pl.pallas_call(
        _expert_mix_kernel,
        grid=(n // G,),
        in_specs=[smem, smem, pl.BlockSpec(tab.shape, lambda i: (0, 0), pipeline_mode=pl.Buffered(1))],
        out_specs=pl.BlockSpec((G * 8, LANES), lambda i: (i, 0)),
        out_shape=jax.ShapeDtypeStruct((n * 8, LANES), jnp.float32),
        compiler_params=pltpu.CompilerParams(vmem_limit_bytes=TABLE_VMEM_LIMIT),
        name="expert_mix",
    )(idx, w, tab)


def _final_kernel(h_ref, pf_ref, g_ref, o_ref):
    p = jnp.concatenate([pf_ref[pl.ds(q, ROW_TILE, stride=FOLD), :] for q in range(FOLD)], axis=1)
    o_ref[...] = _rms(h_ref[...] + p) * g_ref[...]


def _final_norm(h, pf, g):
    n, d = h.shape
    row = pl.BlockSpec((ROW_TILE, d), lambda i: (i, 0))
    return pl.pallas_call(
        _final_kernel,
        grid=(n // ROW_TILE,),
        in_specs=[row, pl.BlockSpec((ROW_TILE * FOLD, LANES), lambda i: (i, 0)), pl.BlockSpec((1, d), lambda i: (0, 0))],
        out_specs=row,
        out_shape=jax.ShapeDtypeStruct((n, d), jnp.float32),
        name="final_norm",
    )(h, pf, g)


def _rope_tables(first_pos, rows):
    half = QK_ROPE // 2
    freqs = ROPE_THETA ** (-jnp.arange(half, dtype=jnp.float32) / half)
    pos = (jnp.arange(rows, dtype=jnp.int32) + first_pos).astype(jnp.float32)
    ang = pos[:, None] * freqs[None, :]
    one = jnp.ones((rows, QK_NOPE), jnp.float32)
    zero = jnp.zeros((rows, HEAD_PAD - QK_NOPE - QK_ROPE), jnp.float32)
    cos = jnp.concatenate([one, jnp.cos(ang), jnp.cos(ang), zero], axis=1)
    sin = jnp.concatenate([0 * one, jnp.sin(ang), jnp.sin(ang), zero], axis=1)
    return cos, sin


def _rope_cols(w):
    half = QK_ROPE // 2
    x1, x2 = w[..., :half], w[..., half:]
    z_front = jnp.zeros(w.shape[:-1] + (QK_NOPE,), w.dtype)
    z_back = jnp.zeros(w.shape[:-1] + (HEAD_PAD - QK_NOPE - QK_ROPE,), w.dtype)
    return (jnp.concatenate([z_front, x1, x2, z_back], axis=-1), jnp.concatenate([z_front, -x2, x1, z_back], axis=-1))


def _prep_weights(w_in, w_uq, w_ukv):
    d = w_in.shape[0]
    c2 = 2 * CONV_CH + Q_LORA + KV_LORA
    kr_a, kr_b = _rope_cols(w_in[:, c2:])
    win = jnp.concatenate([w_in[:, :c2], kr_a, kr_b], axis=1).astype(jnp.bfloat16)
    wq3 = w_uq.reshape(Q_LORA, MLA_HEADS, QK_NOPE + QK_ROPE)
    qa, qb = _rope_cols(wq3[..., QK_NOPE:])
    qa = qa.at[..., :QK_NOPE].set(wq3[..., :QK_NOPE])
    wq = jnp.concatenate([qa.reshape(Q_LORA, -1), qb.reshape(Q_LORA, -1)], axis=1).astype(jnp.bfloat16)
    wkv3 = w_ukv.reshape(KV_LORA, MLA_HEADS, QK_NOPE + V_HEAD)
    zk = jnp.zeros((KV_LORA, MLA_HEADS, HEAD_PAD - QK_NOPE), w_ukv.dtype)
    wk = jnp.concatenate([wkv3[..., :QK_NOPE], zk], axis=-1)
    zv = jnp.zeros((KV_LORA, MLA_HEADS, HEAD_PAD - V_HEAD), w_ukv.dtype)
    v_lo = jnp.concatenate([wkv3[..., QK_NOPE:], zv], axis=-1)
    v_hi = jnp.concatenate([zv, wkv3[..., QK_NOPE:]], axis=-1)
    odd = (jnp.arange(MLA_HEADS) % 2 == 1)[None, :, None]
    wv = jnp.where(odd, v_hi, v_lo)
    wkv = jnp.concatenate([wk.reshape(KV_LORA, -1), wv.reshape(KV_LORA, -1)], axis=1).astype(jnp.bfloat16)
    return win, wq, wkv


def kernel(x, meta, g_mix_norm, w_in, g_q, w_uq, g_kv, w_ukv, conv_w, conv_b, g_conv_ln, b_conv_ln, g_out_attn, g_out_conv, w_out, g_ffn_norm, peer_wq, peer_keys, peer_u, peer_v, g_final):
    B, S, D = x.shape
    assert S % ROW_TILE == 0
    row2 = lambda v: v.reshape(1, -1)
    win, wq, wkv = _prep_weights(w_in[0], w_uq[0], w_ukv[0])
    weights = (row2(g_mix_norm[0]), win, row2(g_q[0]), wq, row2(g_kv[0]), wkv)
    h_meta = jnp.concatenate([jnp.zeros((ZERO_ROWS, D), x.dtype), meta.astype(x.dtype)], axis=0)[None]
    glu_meta, _, k_meta, v_meta = _in_proj(h_meta, weights, *_rope_tables(-ZERO_ROWS, FRONT), FRONT)
    glu, q, k, v = _in_proj(x, weights, *_rope_tables(N_META, S), ROW_TILE)
    convn = _conv_branch(glu_meta, glu, conv_w[0], row2(conv_b[0]), row2(g_conv_ln[0]), row2(b_conv_ln[0]),
                         row2(g_out_conv[0]))
    attn = _attention(q, k_meta, v_meta, k, v)
    n = B * S
    wo = w_out[0].astype(jnp.bfloat16)
    aw = MLA_HEADS * V_HEAD
    h1, xn, xf = _out_proj(x.reshape(n, D), attn.reshape(n, aw), convn.reshape(n, CONV_CH), row2(g_out_attn[0]),
                           wo[:aw], wo[aw:], row2(g_ffn_norm[0]))
    wqk = _fold_keys(peer_keys[0], peer_wq[0])
    idx, gate = _peer_topk(xn, wqk)
    w = _expert_act(_index_blocks(idx), xf, gate, _pack_table(peer_u[0]))
    peer_folded = _expert_mix(idx, w, _pack_table(peer_v[0]))
    return _final_norm(h1, peer_folded, row2(g_final)).reshape(B, S, D)
```

```python
import math
from functools import partial

import jax
import jax.numpy as jnp
from jax import lax
from jax.experimental import pallas as pl
from jax.experimental.pallas import tpu as pltpu

N_META = 16
MLA_HEADS = 8
QK_NOPE = 64
QK_ROPE = 32
V_HEAD = 64
Q_LORA = 256
KV_LORA = 128
ROPE_THETA = 10000.0
CONV_CH = 512
CONV_K = 31
N_KEYS = 128
PEER_HEADS = 8
PEER_DK_HALF = 64
PEER_TOPK = 16
NORM_EPS = 1e-6

LANES = 128
SUBLANES = 8
HEAD_PAD = LANES
FRONT = LANES
ZERO_ROWS = FRONT - N_META
ROW_TILE = 512
CONV_HIST = 32
TOPK_TILE = 512
N_SLOT = PEER_HEADS * PEER_TOPK
ROWS_PER_EXPERT = 4
FOLD = 2 * ROWS_PER_EXPERT
PEER_GROUP = 32
TABLE_PAD = SUBLANES
VMEM_LIMIT = 48 * 1024 * 1024
TABLE_VMEM_LIMIT = 56 * 1024 * 1024

_NT = (((1,), (1,)), ((), ()))


def _rms(x):
    return x * lax.rsqrt(jnp.mean(x * x, axis=-1, keepdims=True) + NORM_EPS)


def _dot(a, b):
    return jnp.dot(a.astype(jnp.bfloat16), b, preferred_element_type=jnp.float32)


def _in_proj_kernel(h_ref, gmix_ref, win_ref, gq_ref, wq_ref, gkv_ref, wkv_ref, cos_ref, sin_ref,
                    glu_ref, q_ref, k_ref, v_ref):
    hn = _rms(h_ref[0]) * gmix_ref[...]
    z = _dot(hn, win_ref[...])
    c0 = 2 * CONV_CH
    c1 = c0 + Q_LORA
    c2 = c1 + KV_LORA
    a, b = z[:, :CONV_CH], z[:, CONV_CH:c0]
    glu_ref[0] = a * (1.0 / (1.0 + jnp.exp(-b)))
    cos, sin = cos_ref[...], sin_ref[...]
    qab = _dot(_rms(z[:, c0:c1]) * gq_ref[...], wq_ref[...])
    kvv = _dot(_rms(z[:, c1:c2]) * gkv_ref[...], wkv_ref[...])
    kr = z[:, c2:c2 + HEAD_PAD] * cos + z[:, c2 + HEAD_PAD:c2 + 2 * HEAD_PAD] * sin
    scale = 1.0 / math.sqrt(QK_NOPE + QK_ROPE)
    hw = MLA_HEADS * HEAD_PAD
    for h in range(MLA_HEADS):
        sl = slice(h * HEAD_PAD, (h + 1) * HEAD_PAD)
        sl2 = slice(hw + h * HEAD_PAD, hw + (h + 1) * HEAD_PAD)
        q_ref[0, h] = ((qab[:, sl] * cos + qab[:, sl2] * sin) * scale).astype(jnp.bfloat16)
        k_ref[0, h] = (kvv[:, sl] + kr).astype(jnp.bfloat16)
        v_ref[0, h] = kvv[:, sl2].astype(jnp.bfloat16)


def _in_proj(h, weights, cos, sin, tile):
    B, rows, D = h.shape
    full = lambda w: pl.BlockSpec(w.shape, lambda b, i: (0,) * w.ndim)
    qkv_shape = jax.ShapeDtypeStruct((B, MLA_HEADS, rows, HEAD_PAD), jnp.bfloat16)
    qkv_spec = pl.BlockSpec((1, MLA_HEADS, tile, HEAD_PAD), lambda b, i: (b, 0, i, 0))
    return pl.pallas_call(
        _in_proj_kernel,
        grid=(B, rows // tile),
        in_specs=[pl.BlockSpec((1, tile, D), lambda b, i: (b, i, 0))] + [full(w) for w in weights] + [
            pl.BlockSpec((tile, HEAD_PAD), lambda b, i: (i, 0)),
            pl.BlockSpec((tile, HEAD_PAD), lambda b, i: (i, 0)),
        ],
        out_specs=[pl.BlockSpec((1, tile, CONV_CH), lambda b, i: (b, i, 0)), qkv_spec, qkv_spec, qkv_spec],
        out_shape=[jax.ShapeDtypeStruct((B, rows, CONV_CH), jnp.float32), qkv_shape, qkv_shape, qkv_shape],
        compiler_params=pltpu.CompilerParams(vmem_limit_bytes=VMEM_LIMIT),
        name="in_proj",
    )(h, *weights, cos, sin)


CONV_CHUNK = 64


def _conv_kernel(mhist_ref, hist_ref, cur_ref, w_ref, b_ref, gln_ref, bln_ref, gout_ref, o_ref, buf_ref, shift_ref):
    buf_ref[0:CONV_HIST] = jnp.where(pl.program_id(1) == 0, mhist_ref[0], hist_ref[0])
    buf_ref[CONV_HIST:] = cur_ref[0]
    base = CONV_HIST - (CONV_K - 1)
    span = ROW_TILE + CONV_HIST - SUBLANES
    for s in range(1, SUBLANES):
        shift_ref[s - 1] = buf_ref[pl.ds(s, span), :]
    for c in range(ROW_TILE // CONV_CHUNK):
        acc = jnp.zeros((CONV_CHUNK, CONV_CH), jnp.float32)
        for k in range(CONV_K):
            phase, start = (base + k) % SUBLANES, c * CONV_CHUNK + (base + k) // SUBLANES * SUBLANES
            src = buf_ref if phase == 0 else shift_ref.at[phase - 1]
            acc = acc + src[pl.ds(start, CONV_CHUNK), :] * w_ref[k:k + 1, :]
        y = acc + b_ref[...]
        mu = jnp.mean(y, axis=-1, keepdims=True)
        yc = y - mu
        var = jnp.mean(yc * yc, axis=-1, keepdims=True)
        y = yc * lax.rsqrt(var + NORM_EPS) * gln_ref[...] + bln_ref[...]
        y = y * (1.0 / (1.0 + jnp.exp(-y)))
        o_ref[0, pl.ds(c * CONV_CHUNK, CONV_CHUNK), :] = (_rms(y) * gout_ref[...]).astype(jnp.bfloat16)


def _conv_branch(glu_meta, glu, w, b, gln, bln, gout):
    B, S, _ = glu.shape
    full = lambda shape: pl.BlockSpec(shape, lambda b, j: (0,) * len(shape))
    per = ROW_TILE // CONV_HIST
    return pl.pallas_call(
        _conv_kernel,
        grid=(B, S // ROW_TILE),
        in_specs=[
            pl.BlockSpec((1, CONV_HIST, CONV_CH), lambda b, j: (0, FRONT // CONV_HIST - 1, 0)),
            pl.BlockSpec((1, CONV_HIST, CONV_CH), lambda b, j: (b, jnp.maximum(j * per - 1, 0), 0)),
            pl.BlockSpec((1, ROW_TILE, CONV_CH), lambda b, j: (b, j, 0)),
            full(w.shape), full(b.shape), full(gln.shape), full(bln.shape), full(gout.shape),
        ],
        out_specs=pl.BlockSpec((1, ROW_TILE, CONV_CH), lambda b, j: (b, j, 0)),
        out_shape=jax.ShapeDtypeStruct((B, S, CONV_CH), jnp.bfloat16),
        scratch_shapes=[pltpu.VMEM((CONV_HIST + ROW_TILE, CONV_CH), jnp.float32),
                        pltpu.VMEM((SUBLANES - 1, CONV_HIST + ROW_TILE - SUBLANES, CONV_CH), jnp.float32)],
        compiler_params=pltpu.CompilerParams(vmem_limit_bytes=VMEM_LIMIT),
        name="conv_branch",
    )(glu_meta, glu, glu, w, b, gln, bln, gout)


def _attn_kernel(q_ref, km_ref, vm_ref, k_ref, v_ref, o_ref):
    qi = pl.program_id(2)
    heads = range(2)
    qs = [q_ref[0, hh] for hh in heads]

    def block(q, kb, vb, mask, carry):
        m, l, acc = carry
        s = lax.dot_general(q, kb, _NT, preferred_element_type=jnp.float32)
        if mask is not None:
            s = jnp.where(mask, s, -jnp.inf)
        m_new = jnp.maximum(m, jnp.max(s, axis=-1, keepdims=True))
        alpha = jnp.exp(m - m_new)
        p = jnp.exp(s - m_new)
        l = alpha * l + jnp.sum(p, axis=-1, keepdims=True)
        acc = alpha * acc + jnp.dot(p.astype(jnp.bfloat16), vb, preferred_element_type=jnp.float32)
        return m_new, l, acc

    def both(kv, mask, carries):
        return tuple(block(qs[hh], kv(k_ref, hh), kv(v_ref, hh), mask, carries[hh]) for hh in heads)

    init = (jnp.full((ROW_TILE, 1), -jnp.inf, jnp.float32), jnp.zeros((ROW_TILE, 1), jnp.float32),
            jnp.zeros((ROW_TILE, HEAD_PAD), jnp.float32))
    col = lax.broadcasted_iota(jnp.int32, (ROW_TILE, FRONT), 1)
    carries = tuple(block(qs[hh], km_ref[0, hh], vm_ref[0, hh], col >= ZERO_ROWS, init) for hh in heads)

    def body(j, carries):
        start = pl.multiple_of(j * ROW_TILE, ROW_TILE)
        return both(lambda ref, hh: ref[0, hh, pl.ds(start, ROW_TILE), :], None, carries)

    carries = lax.fori_loop(0, qi, body, carries)
    start = pl.multiple_of(qi * ROW_TILE, ROW_TILE)
    row = lax.broadcasted_iota(jnp.int32, (ROW_TILE, ROW_TILE), 0)
    colr = lax.broadcasted_iota(jnp.int32, (ROW_TILE, ROW_TILE), 1)
    carries = both(lambda ref, hh: ref[0, hh, pl.ds(start, ROW_TILE), :], colr <= row, carries)
    o_ref[0] = carries[0][2] / carries[0][1] + carries[1][2] / carries[1][1]


def _attention(q, k_meta, v_meta, k, v):
    B, H, S, _ = q.shape
    kv_spec = pl.BlockSpec((1, 2, S, HEAD_PAD), lambda b, hp, i: (b, hp, 0, 0))
    meta_spec = pl.BlockSpec((1, 2, FRONT, HEAD_PAD), lambda b, hp, i: (0, hp, 0, 0))
    return pl.pallas_call(
        _attn_kernel,
        grid=(B, H // 2, S // ROW_TILE),
        in_specs=[pl.BlockSpec((1, 2, ROW_TILE, HEAD_PAD), lambda b, hp, i: (b, hp, i, 0)), meta_spec, meta_spec,
                  kv_spec, kv_spec],
        out_specs=pl.BlockSpec((1, ROW_TILE, HEAD_PAD), lambda b, hp, i: (b, i, hp)),
        out_shape=jax.ShapeDtypeStruct((B, S, H // 2 * HEAD_PAD), jnp.float32),
        compiler_params=pltpu.CompilerParams(vmem_limit_bytes=VMEM_LIMIT),
        name="attention",
    )(q, k_meta, v_meta, k, v)


def _out_proj_kernel(x_ref, attn_ref, conv_ref, gattn_ref, wa_ref, wc_ref, gffn_ref, h_ref, xn_ref, xf_ref):
    an = _rms(attn_ref[...]) * gattn_ref[...]
    h = x_ref[...] + _dot(an, wa_ref[...]) + jnp.dot(conv_ref[...], wc_ref[...], preferred_element_type=jnp.float32)
    h_ref[...] = h
    xn = _rms(h) * gffn_ref[...]
    xn_ref[...] = xn
    for q in range(FOLD):
        xf_ref[pl.ds(q, ROW_TILE, stride=FOLD), :] = xn[:, q * LANES:(q + 1) * LANES]


def _out_proj(x2, attn, convn, gattn, wa, wc, gffn):
    n, d = x2.shape
    full = lambda shape: pl.BlockSpec(shape, lambda i: (0,) * len(shape))
    row = lambda w: pl.BlockSpec((ROW_TILE, w), lambda i: (i, 0))
    return pl.pallas_call(
        _out_proj_kernel,
        grid=(n // ROW_TILE,),
        in_specs=[row(d), row(attn.shape[1]), row(convn.shape[1]), full(gattn.shape), full(wa.shape), full(wc.shape),
                  full(gffn.shape)],
        out_specs=[row(d), row(d), pl.BlockSpec((ROW_TILE * FOLD, LANES), lambda i: (i, 0))],
        out_shape=[jax.ShapeDtypeStruct((n, d), jnp.float32)] * 2 + [jax.ShapeDtypeStruct((n * FOLD, LANES), jnp.float32)],
        compiler_params=pltpu.CompilerParams(vmem_limit_bytes=VMEM_LIMIT),
        name="out_proj",
    )(x2, attn, convn, gattn, wa, wc, gffn)


def _fold_keys_kernel(keys_ref, wqt_ref, o_ref):
    o_ref[0] = jnp.dot(keys_ref[0], wqt_ref[...], preferred_element_type=jnp.float32,
                       precision=lax.Precision.HIGHEST)


def _fold_keys(keys, wq):
    d = wq.shape[0]
    n_grp = 2 * PEER_HEADS
    keys2 = keys.reshape(n_grp, N_KEYS, PEER_DK_HALF)
    src = lambda i: (i % PEER_HEADS) * 2 + i // PEER_HEADS
    folded = pl.pallas_call(
        _fold_keys_kernel,
        grid=(n_grp,),
        in_specs=[pl.BlockSpec((1, N_KEYS, PEER_DK_HALF), lambda i: (src(i), 0, 0)),
                  pl.BlockSpec((PEER_DK_HALF, d), lambda i: (src(i), 0))],
        out_specs=pl.BlockSpec((1, N_KEYS, d), lambda i: (i, 0, 0)),
        out_shape=jax.ShapeDtypeStruct((n_grp, N_KEYS, d), jnp.float32),
        name="fold_keys",
    )(keys2, wq.T)
    return folded.transpose(1, 0, 2).reshape(N_KEYS * n_grp, d).astype(jnp.bfloat16)


def _stair_pairs():
    return [(a, b) for a in range(PEER_TOPK) for b in range(PEER_TOPK) if (a + 1) * (b + 1) <= PEER_TOPK]


def _topk_kernel(xn_ref, wqk_ref, idx_ref, gate_ref, s_ref, sv_ref, si_ref, cv_ref, ci_ref, ts_ref, te_ref):
    T = TOPK_TILE
    n_grp = 2 * PEER_HEADS
    st = lax.dot_general(wqk_ref[...], xn_ref[...].astype(jnp.bfloat16), _NT, preferred_element_type=jnp.float32)
    s_ref[...] = st.reshape(N_KEYS, n_grp, T)
    neg = jnp.float32(-jnp.inf)

    def round1(k, carry):
        m = s_ref[0]
        for n in range(1, N_KEYS):
            m = jnp.maximum(m, s_ref[n])
        arg = jnp.full((n_grp, T), N_KEYS, jnp.int32)
        for n in range(N_KEYS - 1, -1, -1):
            arg = jnp.where(s_ref[n] == m, n, arg)
        for n in range(N_KEYS):
            s_ref[n] = jnp.where(arg == n, neg, s_ref[n])
        sv_ref[k] = m
        si_ref[k] = arg
        return carry

    lax.fori_loop(0, PEER_TOPK, round1, 0)

    pairs = _stair_pairs()
    for c, (a, b) in enumerate(pairs):
        cv_ref[c] = sv_ref[a, 0:PEER_HEADS, :] + sv_ref[b, PEER_HEADS:n_grp, :]
        ci_ref[c] = (si_ref[a, 0:PEER_HEADS, :] * N_KEYS + si_ref[b, PEER_HEADS:n_grp, :]) * ROWS_PER_EXPERT

    def round2(k, carry):
        m = cv_ref[0]
        for c in range(1, len(pairs)):
            m = jnp.maximum(m, cv_ref[c])
        pos = jnp.full((PEER_HEADS, T), len(pairs), jnp.int32)
        for c in range(len(pairs) - 1, -1, -1):
            pos = jnp.where(cv_ref[c] == m, c, pos)
        e = jnp.zeros((PEER_HEADS, T), jnp.int32)
        for c in range(len(pairs)):
            hit = pos == c
            e = jnp.where(hit, ci_ref[c], e)
            cv_ref[c] = jnp.where(hit, neg, cv_ref[c])
        ts_ref[k] = m
        te_ref[k] = e
        return carry

    lax.fori_loop(0, PEER_TOPK, round2, 0)

    ts = ts_ref[...]
    ex = jnp.exp(ts - ts[0:1])
    gate = ex / jnp.sum(ex, axis=0, keepdims=True)
    gate_ref[...] = gate.reshape(N_SLOT, T).T
    idx_ref[...] = te_ref[...].reshape(N_SLOT, T).T


def _peer_topk(xn, wqk):
    n, d = xn.shape
    T = TOPK_TILE
    n_grp = 2 * PEER_HEADS
    n_pairs = len(_stair_pairs())
    return pl.pallas_call(
        _topk_kernel,
        grid=(n // T,),
        in_specs=[pl.BlockSpec((T, d), lambda i: (i, 0)), pl.BlockSpec(wqk.shape, lambda i: (0, 0))],
        out_specs=[pl.BlockSpec((T, N_SLOT), lambda i: (i, 0))] * 2,
        out_shape=[jax.ShapeDtypeStruct((n, N_SLOT), jnp.int32), jax.ShapeDtypeStruct((n, N_SLOT), jnp.float32)],
        scratch_shapes=[
            pltpu.VMEM((N_KEYS, n_grp, T), jnp.float32),
            pltpu.VMEM((PEER_TOPK, n_grp, T), jnp.float32),
            pltpu.VMEM((PEER_TOPK, n_grp, T), jnp.int32),
            pltpu.VMEM((n_pairs, PEER_HEADS, T), jnp.float32),
            pltpu.VMEM((n_pairs, PEER_HEADS, T), jnp.int32),
            pltpu.VMEM((PEER_TOPK, PEER_HEADS, T), jnp.float32),
            pltpu.VMEM((PEER_TOPK, PEER_HEADS, T), jnp.int32),
        ],
        compiler_params=pltpu.CompilerParams(vmem_limit_bytes=VMEM_LIMIT),
        name="peer_topk",
    )(xn, wqk)


def _pack_table(t):
    e, d = t.shape
    bits = lax.bitcast_convert_type(t.astype(jnp.bfloat16), jnp.uint16).astype(jnp.uint32)
    word = (bits[:, d // 2:] << 16) | bits[:, :d // 2]
    rows = lax.bitcast_convert_type(word, jnp.int32).reshape(e * ROWS_PER_EXPERT, LANES)
    return jnp.pad(rows, ((TABLE_PAD, TABLE_PAD), (0, 0)))


def _unpack(word):
    lo = pltpu.bitcast(word << 16, jnp.float32)
    hi = pltpu.bitcast(word & jnp.int32(-65536), jnp.float32)
    return lo, hi


def _split_bf16(x):
    xh = x.astype(jnp.bfloat16)
    return xh, (x - xh.astype(jnp.float32)).astype(jnp.bfloat16)


def _gelu(x):
    return 0.5 * x * (1.0 + lax.erf(x * (1.0 / math.sqrt(2.0))))


def _index_blocks(idx):
    n = idx.shape[0]
    G = PEER_GROUP
    return idx.reshape(n // G, G, N_SLOT).transpose(0, 2, 1).reshape(n // G, 1, N_SLOT * G)


def _with_index_block(idx_hbm, idx_bufs, sem, body):
    i = pl.program_id(0)
    nb = pl.num_programs(0)

    def copy(block, slot):
        return pltpu.make_async_copy(idx_hbm.at[block, 0], idx_bufs[slot], sem.at[slot])

    @pl.when(i == 0)
    def _():
        copy(0, 0).start()

    for slot in range(2):
        @pl.when(i % 2 == slot)
        def _():
            copy(i, slot).wait()

            @pl.when(i + 1 < nb)
            def _():
                copy(i + 1, 1 - slot).start()

            body(idx_bufs[slot])


def _pair_rows(tab_ref, row_a, row_b):
    R = ROWS_PER_EXPERT
    upper = lax.broadcasted_iota(jnp.int32, (SUBLANES, LANES), 0) >= R
    below = tab_ref[pl.ds(pl.multiple_of(row_a, R) + TABLE_PAD, SUBLANES), :]
    above = tab_ref[pl.ds(pl.multiple_of(row_b, R) + (TABLE_PAD - R), SUBLANES), :]
    return jnp.where(upper, above, below)


def _expert_act_kernel(idx_hbm, xf_ref, gate_ref, idxv_ref, tab_ref, cw_ref, p_ref, idx_a, idx_b, sem):
    _with_index_block(idx_hbm, (idx_a, idx_b), sem,
                      partial(_expert_act_body, xf_ref, gate_ref, idxv_ref, tab_ref, cw_ref, p_ref))


def _expert_act_body(xf_ref, gate_ref, idxv_ref, tab_ref, cw_ref, p_ref, idx_ref):
    G = PEER_GROUP
    R = ROWS_PER_EXPERT

    def token(g, carry):
        xg = xf_ref[pl.ds(pl.multiple_of(g * 8, 8), 8), :]
        x_lo = jnp.concatenate([xg[0:R], xg[0:R]], axis=0)
        x_hi = jnp.concatenate([xg[R:2 * R], xg[R:2 * R]], axis=0)
        base = pl.multiple_of(g * (N_SLOT * R), 8)
        for r in range(0, N_SLOT, 2):
            row_a = idx_ref.at[pl.ds(r * G, G)][g]
            row_b = idx_ref.at[pl.ds((r + 1) * G, G)][g]
            lo, hi = _unpack(_pair_rows(tab_ref, row_a, row_b))
            p_ref[pl.ds(base + R * r, 2 * R), :] = lo * x_lo + hi * x_hi
        return carry

    lax.fori_loop(0, G, token, 0)
    q = p_ref[pl.ds(0, G * N_SLOT, stride=R), :]
    for k in range(1, R):
        q = q + p_ref[pl.ds(k, G * N_SLOT, stride=R), :]
    qh, ql = _split_bf16(q)
    ones = jnp.ones((8, LANES), jnp.bfloat16)
    s = (lax.dot_general(ones, qh, _NT, preferred_element_type=jnp.float32)
         + lax.dot_general(ones, ql, _NT, preferred_element_type=jnp.float32))
    for g in range(G):
        w = gate_ref[g:g + 1, :] * _gelu(s[0:1, g * N_SLOT:(g + 1) * N_SLOT])
        wbits = pltpu.bitcast(w.astype(jnp.bfloat16).astype(jnp.float32), jnp.int32)
        cw_ref[g:g + 1, :] = (wbits & jnp.int32(-65536)) | idxv_ref[g:g + 1, :]


def _expert_act(idx_blocks, xf, gate, idx, tab):
    G = PEER_GROUP
    n = idx_blocks.shape[0] * G
    return pl.pallas_call(
        _expert_act_kernel,
        grid=(n // G,),
        in_specs=[
            pl.BlockSpec(memory_space=pl.ANY),
            pl.BlockSpec((G * 8, LANES), lambda i: (i, 0)),
            pl.BlockSpec((G, N_SLOT), lambda i: (i, 0)),
            pl.BlockSpec((G, N_SLOT), lambda i: (i, 0)),
            pl.BlockSpec(tab.shape, lambda i: (0, 0), pipeline_mode=pl.Buffered(1)),
        ],
        out_specs=pl.BlockSpec((G, N_SLOT), lambda i: (i, 0)),
        out_shape=jax.ShapeDtypeStruct((n, N_SLOT), jnp.int32),
        scratch_shapes=[pltpu.VMEM((G * N_SLOT * ROWS_PER_EXPERT, LANES), jnp.float32),
                        pltpu.SMEM((N_SLOT * G,), jnp.int32), pltpu.SMEM((N_SLOT * G,), jnp.int32),
                        pltpu.SemaphoreType.DMA((2,))],
        compiler_params=pltpu.CompilerParams(vmem_limit_bytes=TABLE_VMEM_LIMIT),
        name="expert_act",
    )(idx_blocks, xf, gate, idx, tab)


def _expert_mix_kernel(cw_hbm, tab_ref, o_ref, cw_a, cw_b, sem):
    _with_index_block(cw_hbm, (cw_a, cw_b), sem, partial(_expert_mix_body, tab_ref, o_ref))


def _expert_mix_body(tab_ref, o_ref, cw_ref):
    G = PEER_GROUP
    R = ROWS_PER_EXPERT
    upper = lax.broadcasted_iota(jnp.int32, (SUBLANES, LANES), 0) >= R

    def token(g, carry):
        acc_lo = jnp.zeros((SUBLANES, LANES), jnp.float32)
        acc_hi = jnp.zeros((SUBLANES, LANES), jnp.float32)
        for r in range(0, N_SLOT, 2):
            cw_a = cw_ref.at[pl.ds(r * G, G)][g]
            cw_b = cw_ref.at[pl.ds((r + 1) * G, G)][g]
            lo, hi = _unpack(_pair_rows(tab_ref, cw_a & 0xFFFF, cw_b & 0xFFFF))
            w = pltpu.bitcast(jnp.where(upper, cw_b, cw_a) & jnp.int32(-65536), jnp.float32)
            acc_lo = acc_lo + w * lo
            acc_hi = acc_hi + w * hi
        o_ref[pl.ds(pl.multiple_of(g * 8, 8), 8), :] = jnp.concatenate(
            [acc_lo[0:R] + acc_lo[R:], acc_hi[0:R] + acc_hi[R:]], axis=0)
        return carry

    lax.fori_loop(0, G, token, 0)


def _expert_mix(cw_blocks, tab):
    G = PEER_GROUP
    nb = cw_blocks.shape[0]
    return pl.pallas_call(
        _expert_mix_kernel,
        grid=(nb,),
        in_specs=[pl.BlockSpec(memory_space=pl.ANY),
                  pl.BlockSpec(tab.shape, lambda i: (0, 0), pipeline_mode=pl.Buffered(1))],
        out_specs=pl.BlockSpec((G * 8, LANES), lambda i: (i, 0)),
        out_shape=jax.ShapeDtypeStruct((nb * G * 8, LANES), jnp.float32),
        scratch_shapes=[pltpu.SMEM((N_SLOT * G,), jnp.int32), pltpu.SMEM((N_SLOT * G,), jnp.int32),
                        pltpu.SemaphoreType.DMA((2,))],
        compiler_params=pltpu.CompilerParams(vmem_limit_bytes=TABLE_VMEM_LIMIT),
        name="expert_mix",
    )(cw_blocks, tab)


def _final_kernel(h_ref, pf_ref, g_ref, o_ref):
    p = jnp.concatenate([pf_ref[pl.ds(q, ROW_TILE, stride=FOLD), :] for q in range(FOLD)], axis=1)
    o_ref[...] = _rms(h_ref[...] + p) * g_ref[...]


def _final_norm(h, pf, g):
    n, d = h.shape
    row = pl.BlockSpec((ROW_TILE, d), lambda i: (i, 0))
    return pl.pallas_call(
        _final_kernel,
        grid=(n // ROW_TILE,),
        in_specs=[row, pl.BlockSpec((ROW_TILE * FOLD, LANES), lambda i: (i, 0)), pl.BlockSpec((1, d), lambda i: (0, 0))],
        out_specs=row,
        out_shape=jax.ShapeDtypeStruct((n, d), jnp.float32),
        name="final_norm",
    )(h, pf, g)


def _rope_tables(first_pos, rows):
    half = QK_ROPE // 2
    freqs = ROPE_THETA ** (-jnp.arange(half, dtype=jnp.float32) / half)
    pos = (jnp.arange(rows, dtype=jnp.int32) + first_pos).astype(jnp.float32)
    ang = pos[:, None] * freqs[None, :]
    one = jnp.ones((rows, QK_NOPE), jnp.float32)
    zero = jnp.zeros((rows, HEAD_PAD - QK_NOPE - QK_ROPE), jnp.float32)
    cos = jnp.concatenate([one, jnp.cos(ang), jnp.cos(ang), zero], axis=1)
    sin = jnp.concatenate([0 * one, jnp.sin(ang), jnp.sin(ang), zero], axis=1)
    return cos, sin


def _rope_cols(w):
    half = QK_ROPE // 2
    x1, x2 = w[..., :half], w[..., half:]
    z_front = jnp.zeros(w.shape[:-1] + (QK_NOPE,), w.dtype)
    z_back = jnp.zeros(w.shape[:-1] + (HEAD_PAD - QK_NOPE - QK_ROPE,), w.dtype)
    return (jnp.concatenate([z_front, x1, x2, z_back], axis=-1), jnp.concatenate([z_front, -x2, x1, z_back], axis=-1))


def _prep_weights(w_in, w_uq, w_ukv):
    d = w_in.shape[0]
    c2 = 2 * CONV_CH + Q_LORA + KV_LORA
    kr_a, kr_b = _rope_cols(w_in[:, c2:])
    win = jnp.concatenate([w_in[:, :c2], kr_a, kr_b], axis=1).astype(jnp.bfloat16)
    wq3 = w_uq.reshape(Q_LORA, MLA_HEADS, QK_NOPE + QK_ROPE)
    qa, qb = _rope_cols(wq3[..., QK_NOPE:])
    qa = qa.at[..., :QK_NOPE].set(wq3[..., :QK_NOPE])
    wq = jnp.concatenate([qa.reshape(Q_LORA, -1), qb.reshape(Q_LORA, -1)], axis=1).astype(jnp.bfloat16)
    wkv3 = w_ukv.reshape(KV_LORA, MLA_HEADS, QK_NOPE + V_HEAD)
    zk = jnp.zeros((KV_LORA, MLA_HEADS, HEAD_PAD - QK_NOPE), w_ukv.dtype)
    wk = jnp.concatenate([wkv3[..., :QK_NOPE], zk], axis=-1)
    zv = jnp.zeros((KV_LORA, MLA_HEADS, HEAD_PAD - V_HEAD), w_ukv.dtype)
    v_lo = jnp.concatenate([wkv3[..., QK_NOPE:], zv], axis=-1)
    v_hi = jnp.concatenate([zv, wkv3[..., QK_NOPE:]], axis=-1)
    odd = (jnp.arange(MLA_HEADS) % 2 == 1)[None, :, None]
    wv = jnp.where(odd, v_hi, v_lo)
    wkv = jnp.concatenate([wk.reshape(KV_LORA, -1), wv.reshape(KV_LORA, -1)], axis=1).astype(jnp.bfloat16)
    return win, wq, wkv


def kernel(x, meta, g_mix_norm, w_in, g_q, w_uq, g_kv, w_ukv, conv_w, conv_b, g_conv_ln, b_conv_ln, g_out_attn, g_out_conv, w_out, g_ffn_norm, peer_wq, peer_keys, peer_u, peer_v, g_final):
    B, S, D = x.shape
    assert S % ROW_TILE == 0
    row2 = lambda v: v.reshape(1, -1)
    win, wq, wkv = _prep_weights(w_in[0], w_uq[0], w_ukv[0])
    weights = (row2(g_mix_norm[0]), win, row2(g_q[0]), wq, row2(g_kv[0]), wkv)
    h_meta = jnp.concatenate([jnp.zeros((ZERO_ROWS, D), x.dtype), meta.astype(x.dtype)], axis=0)[None]
    glu_meta, _, k_meta, v_meta = _in_proj(h_meta, weights, *_rope_tables(-ZERO_ROWS, FRONT), FRONT)
    glu, q, k, v = _in_proj(x, weights, *_rope_tables(N_META, S), ROW_TILE)
    convn = _conv_branch(glu_meta, glu, conv_w[0], row2(conv_b[0]), row2(g_conv_ln[0]), row2(b_conv_ln[0]),
                         row2(g_out_conv[0]))
    attn = _attention(q, k_meta, v_meta, k, v)
    n = B * S
    wo = w_out[0].astype(jnp.bfloat16)
    aw = MLA_HEADS * V_HEAD
    h1, xn, xf = _out_proj(x.reshape(n, D), attn.reshape(n, aw), convn.reshape(n, CONV_CH), row2(g_out_attn[0]),
                           wo[:aw], wo[aw:], row2(g_ffn_norm[0]))
    wqk = _fold_keys(peer_keys[0], peer_wq[0])
    idx, gate = _peer_topk(xn, wqk)
    slot_words = _expert_act(_index_blocks(idx), xf, gate, idx, _pack_table(peer_u[0]))
    peer_folded = _expert_mix(_index_blocks(slot_words), _pack_table(peer_v[0]))
    return _final_norm(h1, peer_folded, row2(g_final)).reshape(B, S, D)
```

```python
import math
from functools import partial

import jax
import jax.numpy as jnp
from jax import lax
from jax.experimental import pallas as pl
from jax.experimental.pallas import tpu as pltpu

N_META = 16
MLA_HEADS = 8
QK_NOPE = 64
QK_ROPE = 32
V_HEAD = 64
Q_LORA = 256
KV_LORA = 128
ROPE_THETA = 10000.0
CONV_CH = 512
CONV_K = 31
N_KEYS = 128
PEER_HEADS = 8
PEER_DK_HALF = 64
PEER_TOPK = 16
NORM_EPS = 1e-6

LANES = 128
SUBLANES = 8
HEAD_PAD = LANES
FRONT = LANES
ZERO_ROWS = FRONT - N_META
ROW_TILE = 512
CONV_HIST = 32
TOPK_TILE = 512
N_SLOT = PEER_HEADS * PEER_TOPK
ROWS_PER_EXPERT = 4
FOLD = 2 * ROWS_PER_EXPERT
PEER_GROUP = 64
TABLE_PAD = SUBLANES
VMEM_LIMIT = 48 * 1024 * 1024
TABLE_VMEM_LIMIT = 56 * 1024 * 1024

_NT = (((1,), (1,)), ((), ()))


def _rms(x):
    return x * lax.rsqrt(jnp.mean(x * x, axis=-1, keepdims=True) + NORM_EPS)


def _dot(a, b):
    return jnp.dot(a.astype(jnp.bfloat16), b, preferred_element_type=jnp.float32)


def _in_proj_kernel(h_ref, gmix_ref, win_ref, gq_ref, wq_ref, gkv_ref, wkv_ref, cos_ref, sin_ref,
                    glu_ref, q_ref, k_ref, v_ref):
    hn = _rms(h_ref[0]) * gmix_ref[...]
    z = _dot(hn, win_ref[...])
    c0 = 2 * CONV_CH
    c1 = c0 + Q_LORA
    c2 = c1 + KV_LORA
    a, b = z[:, :CONV_CH], z[:, CONV_CH:c0]
    glu_ref[0] = a * (1.0 / (1.0 + jnp.exp(-b)))
    cos, sin = cos_ref[...], sin_ref[...]
    qab = _dot(_rms(z[:, c0:c1]) * gq_ref[...], wq_ref[...])
    kvv = _dot(_rms(z[:, c1:c2]) * gkv_ref[...], wkv_ref[...])
    kr = z[:, c2:c2 + HEAD_PAD] * cos + z[:, c2 + HEAD_PAD:c2 + 2 * HEAD_PAD] * sin
    scale = 1.0 / math.sqrt(QK_NOPE + QK_ROPE)
    hw = MLA_HEADS * HEAD_PAD
    for h in range(MLA_HEADS):
        sl = slice(h * HEAD_PAD, (h + 1) * HEAD_PAD)
        sl2 = slice(hw + h * HEAD_PAD, hw + (h + 1) * HEAD_PAD)
        q_ref[0, h] = ((qab[:, sl] * cos + qab[:, sl2] * sin) * scale).astype(jnp.bfloat16)
        k_ref[0, h] = (kvv[:, sl] + kr).astype(jnp.bfloat16)
        v_ref[0, h] = kvv[:, sl2].astype(jnp.bfloat16)


def _in_proj(h, weights, cos, sin, tile):
    B, rows, D = h.shape
    full = lambda w: pl.BlockSpec(w.shape, lambda b, i: (0,) * w.ndim)
    qkv_shape = jax.ShapeDtypeStruct((B, MLA_HEADS, rows, HEAD_PAD), jnp.bfloat16)
    qkv_spec = pl.BlockSpec((1, MLA_HEADS, tile, HEAD_PAD), lambda b, i: (b, 0, i, 0))
    return pl.pallas_call(
        _in_proj_kernel,
        grid=(B, rows // tile),
        in_specs=[pl.BlockSpec((1, tile, D), lambda b, i: (b, i, 0))] + [full(w) for w in weights] + [
            pl.BlockSpec((tile, HEAD_PAD), lambda b, i: (i, 0)),
            pl.BlockSpec((tile, HEAD_PAD), lambda b, i: (i, 0)),
        ],
        out_specs=[pl.BlockSpec((1, tile, CONV_CH), lambda b, i: (b, i, 0)), qkv_spec, qkv_spec, qkv_spec],
        out_shape=[jax.ShapeDtypeStruct((B, rows, CONV_CH), jnp.float32), qkv_shape, qkv_shape, qkv_shape],
        compiler_params=pltpu.CompilerParams(vmem_limit_bytes=VMEM_LIMIT),
        name="in_proj",
    )(h, *weights, cos, sin)


CONV_CHUNK = 64


def _conv_kernel(mhist_ref, hist_ref, cur_ref, w_ref, b_ref, gln_ref, bln_ref, gout_ref, o_ref, buf_ref, shift_ref):
    buf_ref[0:CONV_HIST] = jnp.where(pl.program_id(1) == 0, mhist_ref[0], hist_ref[0])
    buf_ref[CONV_HIST:] = cur_ref[0]
    base = CONV_HIST - (CONV_K - 1)
    span = ROW_TILE + CONV_HIST - SUBLANES
    for s in range(1, SUBLANES):
        shift_ref[s - 1] = buf_ref[pl.ds(s, span), :]
    for c in range(ROW_TILE // CONV_CHUNK):
        acc = jnp.zeros((CONV_CHUNK, CONV_CH), jnp.float32)
        for k in range(CONV_K):
            phase, start = (base + k) % SUBLANES, c * CONV_CHUNK + (base + k) // SUBLANES * SUBLANES
            src = buf_ref if phase == 0 else shift_ref.at[phase - 1]
            acc = acc + src[pl.ds(start, CONV_CHUNK), :] * w_ref[k:k + 1, :]
        y = acc + b_ref[...]
        mu = jnp.mean(y, axis=-1, keepdims=True)
        yc = y - mu
        var = jnp.mean(yc * yc, axis=-1, keepdims=True)
        y = yc * lax.rsqrt(var + NORM_EPS) * gln_ref[...] + bln_ref[...]
        y = y * (1.0 / (1.0 + jnp.exp(-y)))
        o_ref[0, pl.ds(c * CONV_CHUNK, CONV_CHUNK), :] = (_rms(y) * gout_ref[...]).astype(jnp.bfloat16)


def _conv_branch(glu_meta, glu, w, b, gln, bln, gout):
    B, S, _ = glu.shape
    full = lambda shape: pl.BlockSpec(shape, lambda b, j: (0,) * len(shape))
    per = ROW_TILE // CONV_HIST
    return pl.pallas_call(
        _conv_kernel,
        grid=(B, S // ROW_TILE),
        in_specs=[
            pl.BlockSpec((1, CONV_HIST, CONV_CH), lambda b, j: (0, FRONT // CONV_HIST - 1, 0)),
            pl.BlockSpec((1, CONV_HIST, CONV_CH), lambda b, j: (b, jnp.maximum(j * per - 1, 0), 0)),
            pl.BlockSpec((1, ROW_TILE, CONV_CH), lambda b, j: (b, j, 0)),
            full(w.shape), full(b.shape), full(gln.shape), full(bln.shape), full(gout.shape),
        ],
        out_specs=pl.BlockSpec((1, ROW_TILE, CONV_CH), lambda b, j: (b, j, 0)),
        out_shape=jax.ShapeDtypeStruct((B, S, CONV_CH), jnp.bfloat16),
        scratch_shapes=[pltpu.VMEM((CONV_HIST + ROW_TILE, CONV_CH), jnp.float32),
                        pltpu.VMEM((SUBLANES - 1, CONV_HIST + ROW_TILE - SUBLANES, CONV_CH), jnp.float32)],
        compiler_params=pltpu.CompilerParams(vmem_limit_bytes=VMEM_LIMIT),
        name="conv_branch",
    )(glu_meta, glu, glu, w, b, gln, bln, gout)


def _attn_kernel(q_ref, km_ref, vm_ref, k_ref, v_ref, o_ref):
    qi = pl.program_id(2)
    heads = range(2)
    qs = [q_ref[0, hh] for hh in heads]

    def block(q, kb, vb, mask, carry):
        m, l, acc = carry
        s = lax.dot_general(q, kb, _NT, preferred_element_type=jnp.float32)
        if mask is not None:
            s = jnp.where(mask, s, -jnp.inf)
        m_new = jnp.maximum(m, jnp.max(s, axis=-1, keepdims=True))
        alpha = jnp.exp(m - m_new)
        p = jnp.exp(s - m_new)
        l = alpha * l + jnp.sum(p, axis=-1, keepdims=True)
        acc = alpha * acc + jnp.dot(p.astype(jnp.bfloat16), vb, preferred_element_type=jnp.float32)
        return m_new, l, acc

    def both(kv, mask, carries):
        return tuple(block(qs[hh], kv(k_ref, hh), kv(v_ref, hh), mask, carries[hh]) for hh in heads)

    init = (jnp.full((ROW_TILE, 1), -jnp.inf, jnp.float32), jnp.zeros((ROW_TILE, 1), jnp.float32),
            jnp.zeros((ROW_TILE, HEAD_PAD), jnp.float32))
    col = lax.broadcasted_iota(jnp.int32, (ROW_TILE, FRONT), 1)
    carries = tuple(block(qs[hh], km_ref[0, hh], vm_ref[0, hh], col >= ZERO_ROWS, init) for hh in heads)

    def body(j, carries):
        start = pl.multiple_of(j * ROW_TILE, ROW_TILE)
        return both(lambda ref, hh: ref[0, hh, pl.ds(start, ROW_TILE), :], None, carries)

    carries = lax.fori_loop(0, qi, body, carries)
    start = pl.multiple_of(qi * ROW_TILE, ROW_TILE)
    row = lax.broadcasted_iota(jnp.int32, (ROW_TILE, ROW_TILE), 0)
    colr = lax.broadcasted_iota(jnp.int32, (ROW_TILE, ROW_TILE), 1)
    carries = both(lambda ref, hh: ref[0, hh, pl.ds(start, ROW_TILE), :], colr <= row, carries)
    o_ref[0] = carries[0][2] / carries[0][1] + carries[1][2] / carries[1][1]


def _attention(q, k_meta, v_meta, k, v):
    B, H, S, _ = q.shape
    kv_spec = pl.BlockSpec((1, 2, S, HEAD_PAD), lambda b, hp, i: (b, hp, 0, 0))
    meta_spec = pl.BlockSpec((1, 2, FRONT, HEAD_PAD), lambda b, hp, i: (0, hp, 0, 0))
    return pl.pallas_call(
        _attn_kernel,
        grid=(B, H // 2, S // ROW_TILE),
        in_specs=[pl.BlockSpec((1, 2, ROW_TILE, HEAD_PAD), lambda b, hp, i: (b, hp, i, 0)), meta_spec, meta_spec,
                  kv_spec, kv_spec],
        out_specs=pl.BlockSpec((1, ROW_TILE, HEAD_PAD), lambda b, hp, i: (b, i, hp)),
        out_shape=jax.ShapeDtypeStruct((B, S, H // 2 * HEAD_PAD), jnp.float32),
        compiler_params=pltpu.CompilerParams(vmem_limit_bytes=VMEM_LIMIT),
        name="attention",
    )(q, k_meta, v_meta, k, v)


def _out_proj_kernel(x_ref, attn_ref, conv_ref, gattn_ref, wa_ref, wc_ref, gffn_ref, h_ref, xn_ref, xf_ref):
    an = _rms(attn_ref[...]) * gattn_ref[...]
    h = x_ref[...] + _dot(an, wa_ref[...]) + jnp.dot(conv_ref[...], wc_ref[...], preferred_element_type=jnp.float32)
    h_ref[...] = h
    xn = _rms(h) * gffn_ref[...]
    xn_ref[...] = xn
    for q in range(FOLD):
        xf_ref[pl.ds(q, ROW_TILE, stride=FOLD), :] = xn[:, q * LANES:(q + 1) * LANES]


def _out_proj(x2, attn, convn, gattn, wa, wc, gffn):
    n, d = x2.shape
    full = lambda shape: pl.BlockSpec(shape, lambda i: (0,) * len(shape))
    row = lambda w: pl.BlockSpec((ROW_TILE, w), lambda i: (i, 0))
    return pl.pallas_call(
        _out_proj_kernel,
        grid=(n // ROW_TILE,),
        in_specs=[row(d), row(attn.shape[1]), row(convn.shape[1]), full(gattn.shape), full(wa.shape), full(wc.shape),
                  full(gffn.shape)],
        out_specs=[row(d), row(d), pl.BlockSpec((ROW_TILE * FOLD, LANES), lambda i: (i, 0))],
        out_shape=[jax.ShapeDtypeStruct((n, d), jnp.float32)] * 2 + [jax.ShapeDtypeStruct((n * FOLD, LANES), jnp.float32)],
        compiler_params=pltpu.CompilerParams(vmem_limit_bytes=VMEM_LIMIT),
        name="out_proj",
    )(x2, attn, convn, gattn, wa, wc, gffn)


def _fold_keys_kernel(keys_ref, wqt_ref, o_ref):
    o_ref[0] = jnp.dot(keys_ref[0], wqt_ref[...], preferred_element_type=jnp.float32,
                       precision=lax.Precision.HIGHEST)


def _fold_keys(keys, wq):
    d = wq.shape[0]
    n_grp = 2 * PEER_HEADS
    keys2 = keys.reshape(n_grp, N_KEYS, PEER_DK_HALF)
    src = lambda i: (i % PEER_HEADS) * 2 + i // PEER_HEADS
    folded = pl.pallas_call(
        _fold_keys_kernel,
        grid=(n_grp,),
        in_specs=[pl.BlockSpec((1, N_KEYS, PEER_DK_HALF), lambda i: (src(i), 0, 0)),
                  pl.BlockSpec((PEER_DK_HALF, d), lambda i: (src(i), 0))],
        out_specs=pl.BlockSpec((1, N_KEYS, d), lambda i: (i, 0, 0)),
        out_shape=jax.ShapeDtypeStruct((n_grp, N_KEYS, d), jnp.float32),
        name="fold_keys",
    )(keys2, wq.T)
    return folded.transpose(1, 0, 2).reshape(N_KEYS * n_grp, d).astype(jnp.bfloat16)


def _stair_pairs():
    return [(a, b) for a in range(PEER_TOPK) for b in range(PEER_TOPK) if (a + 1) * (b + 1) <= PEER_TOPK]


def _topk_kernel(xn_ref, wqk_ref, idx_ref, gate_ref, s_ref, sv_ref, si_ref, cv_ref, ci_ref, ts_ref, te_ref):
    T = TOPK_TILE
    n_grp = 2 * PEER_HEADS
    st = lax.dot_general(wqk_ref[...], xn_ref[...].astype(jnp.bfloat16), _NT, preferred_element_type=jnp.float32)
    s_ref[...] = st.reshape(N_KEYS, n_grp, T)
    neg = jnp.float32(-jnp.inf)

    def round1(k, carry):
        m = s_ref[0]
        for n in range(1, N_KEYS):
            m = jnp.maximum(m, s_ref[n])
        arg = jnp.full((n_grp, T), N_KEYS, jnp.int32)
        for n in range(N_KEYS - 1, -1, -1):
            arg = jnp.where(s_ref[n] == m, n, arg)
        for n in range(N_KEYS):
            s_ref[n] = jnp.where(arg == n, neg, s_ref[n])
        sv_ref[k] = m
        si_ref[k] = arg
        return carry

    lax.fori_loop(0, PEER_TOPK, round1, 0)

    pairs = _stair_pairs()
    for c, (a, b) in enumerate(pairs):
        cv_ref[c] = sv_ref[a, 0:PEER_HEADS, :] + sv_ref[b, PEER_HEADS:n_grp, :]
        ci_ref[c] = (si_ref[a, 0:PEER_HEADS, :] * N_KEYS + si_ref[b, PEER_HEADS:n_grp, :]) * ROWS_PER_EXPERT

    def round2(k, carry):
        m = cv_ref[0]
        for c in range(1, len(pairs)):
            m = jnp.maximum(m, cv_ref[c])
        pos = jnp.full((PEER_HEADS, T), len(pairs), jnp.int32)
        for c in range(len(pairs) - 1, -1, -1):
            pos = jnp.where(cv_ref[c] == m, c, pos)
        e = jnp.zeros((PEER_HEADS, T), jnp.int32)
        for c in range(len(pairs)):
            hit = pos == c
            e = jnp.where(hit, ci_ref[c], e)
            cv_ref[c] = jnp.where(hit, neg, cv_ref[c])
        ts_ref[k] = m
        te_ref[k] = e
        return carry

    lax.fori_loop(0, PEER_TOPK, round2, 0)

    ts = ts_ref[...]
    ex = jnp.exp(ts - ts[0:1])
    gate = ex / jnp.sum(ex, axis=0, keepdims=True)
    gate_ref[...] = gate.reshape(N_SLOT, T).T
    idx_ref[...] = te_ref[...].reshape(N_SLOT, T).T


def _peer_topk(xn, wqk):
    n, d = xn.shape
    T = TOPK_TILE
    n_grp = 2 * PEER_HEADS
    n_pairs = len(_stair_pairs())
    return pl.pallas_call(
        _topk_kernel,
        grid=(n // T,),
        in_specs=[pl.BlockSpec((T, d), lambda i: (i, 0)), pl.BlockSpec(wqk.shape, lambda i: (0, 0))],
        out_specs=[pl.BlockSpec((T, N_SLOT), lambda i: (i, 0))] * 2,
        out_shape=[jax.ShapeDtypeStruct((n, N_SLOT), jnp.int32), jax.ShapeDtypeStruct((n, N_SLOT), jnp.float32)],
        scratch_shapes=[
            pltpu.VMEM((N_KEYS, n_grp, T), jnp.float32),
            pltpu.VMEM((PEER_TOPK, n_grp, T), jnp.float32),
            pltpu.VMEM((PEER_TOPK, n_grp, T), jnp.int32),
            pltpu.VMEM((n_pairs, PEER_HEADS, T), jnp.float32),
            pltpu.VMEM((n_pairs, PEER_HEADS, T), jnp.int32),
            pltpu.VMEM((PEER_TOPK, PEER_HEADS, T), jnp.float32),
            pltpu.VMEM((PEER_TOPK, PEER_HEADS, T), jnp.int32),
        ],
        compiler_params=pltpu.CompilerParams(vmem_limit_bytes=VMEM_LIMIT),
        name="peer_topk",
    )(xn, wqk)


def _pack_table(t):
    e, d = t.shape
    bits = lax.bitcast_convert_type(t.astype(jnp.bfloat16), jnp.uint16).astype(jnp.uint32)
    word = (bits[:, d // 2:] << 16) | bits[:, :d // 2]
    rows = lax.bitcast_convert_type(word, jnp.int32).reshape(e * ROWS_PER_EXPERT, LANES)
    return jnp.pad(rows, ((TABLE_PAD, TABLE_PAD), (0, 0)))


def _unpack(word):
    lo = pltpu.bitcast(word << 16, jnp.float32)
    hi = pltpu.bitcast(word & jnp.int32(-65536), jnp.float32)
    return lo, hi


def _split_bf16(x):
    xh = x.astype(jnp.bfloat16)
    return xh, (x - xh.astype(jnp.float32)).astype(jnp.bfloat16)


def _gelu(x):
    return 0.5 * x * (1.0 + lax.erf(x * (1.0 / math.sqrt(2.0))))


def _index_blocks(idx):
    n = idx.shape[0]
    G = PEER_GROUP
    return idx.reshape(n // G, G, N_SLOT).transpose(0, 2, 1).reshape(n // G, 1, N_SLOT * G)


def _with_index_block(idx_hbm, idx_bufs, sem, body):
    i = pl.program_id(0)
    nb = pl.num_programs(0)

    def copy(block, slot):
        return pltpu.make_async_copy(idx_hbm.at[block, 0], idx_bufs[slot], sem.at[slot])

    @pl.when(i == 0)
    def _():
        copy(0, 0).start()

    for slot in range(2):
        @pl.when(i % 2 == slot)
        def _():
            copy(i, slot).wait()

            @pl.when(i + 1 < nb)
            def _():
                copy(i + 1, 1 - slot).start()

            body(idx_bufs[slot])


def _pair_rows(tab_ref, row_a, row_b):
    R = ROWS_PER_EXPERT
    upper = lax.broadcasted_iota(jnp.int32, (SUBLANES, LANES), 0) >= R
    below = tab_ref[pl.ds(pl.multiple_of(row_a, R) + TABLE_PAD, SUBLANES), :]
    above = tab_ref[pl.ds(pl.multiple_of(row_b, R) + (TABLE_PAD - R), SUBLANES), :]
    return jnp.where(upper, above, below)


def _expert_act_kernel(idx_hbm, xf_ref, gate_ref, idxv_ref, tab_ref, cw_ref, p_ref, idx_a, idx_b, sem):
    _with_index_block(idx_hbm, (idx_a, idx_b), sem,
                      partial(_expert_act_body, xf_ref, gate_ref, idxv_ref, tab_ref, cw_ref, p_ref))


def _expert_act_body(xf_ref, gate_ref, idxv_ref, tab_ref, cw_ref, p_ref, idx_ref):
    G = PEER_GROUP
    R = ROWS_PER_EXPERT

    def token(g, carry):
        xg = xf_ref[pl.ds(pl.multiple_of(g * 8, 8), 8), :]
        x_lo = jnp.concatenate([xg[0:R], xg[0:R]], axis=0)
        x_hi = jnp.concatenate([xg[R:2 * R], xg[R:2 * R]], axis=0)
        base = pl.multiple_of(g * (N_SLOT * R), 8)
        for r in range(0, N_SLOT, 2):
            row_a = idx_ref.at[pl.ds(r * G, G)][g]
            row_b = idx_ref.at[pl.ds((r + 1) * G, G)][g]
            lo, hi = _unpack(_pair_rows(tab_ref, row_a, row_b))
            p_ref[pl.ds(base + R * r, 2 * R), :] = lo * x_lo + hi * x_hi
        return carry

    lax.fori_loop(0, G, token, 0)
    q = p_ref[pl.ds(0, G * N_SLOT, stride=R), :]
    for k in range(1, R):
        q = q + p_ref[pl.ds(k, G * N_SLOT, stride=R), :]
    qh, ql = _split_bf16(q)
    ones = jnp.ones((8, LANES), jnp.bfloat16)
    s = (lax.dot_general(ones, qh, _NT, preferred_element_type=jnp.float32)
         + lax.dot_general(ones, ql, _NT, preferred_element_type=jnp.float32))
    for g in range(G):
        w = gate_ref[g:g + 1, :] * _gelu(s[0:1, g * N_SLOT:(g + 1) * N_SLOT])
        wbits = pltpu.bitcast(w.astype(jnp.bfloat16).astype(jnp.float32), jnp.int32)
        cw_ref[g:g + 1, :] = (wbits & jnp.int32(-65536)) | idxv_ref[g:g + 1, :]


def _expert_act(idx_blocks, xf, gate, idx, tab):
    G = PEER_GROUP
    n = idx_blocks.shape[0] * G
    return pl.pallas_call(
        _expert_act_kernel,
        grid=(n // G,),
        in_specs=[
            pl.BlockSpec(memory_space=pl.ANY),
            pl.BlockSpec((G * 8, LANES), lambda i: (i, 0)),
            pl.BlockSpec((G, N_SLOT), lambda i: (i, 0)),
            pl.BlockSpec((G, N_SLOT), lambda i: (i, 0)),
            pl.BlockSpec(tab.shape, lambda i: (0, 0), pipeline_mode=pl.Buffered(1)),
        ],
        out_specs=pl.BlockSpec((G, N_SLOT), lambda i: (i, 0)),
        out_shape=jax.ShapeDtypeStruct((n, N_SLOT), jnp.int32),
        scratch_shapes=[pltpu.VMEM((G * N_SLOT * ROWS_PER_EXPERT, LANES), jnp.float32),
                        pltpu.SMEM((N_SLOT * G,), jnp.int32), pltpu.SMEM((N_SLOT * G,), jnp.int32),
                        pltpu.SemaphoreType.DMA((2,))],
        compiler_params=pltpu.CompilerParams(vmem_limit_bytes=TABLE_VMEM_LIMIT),
        name="expert_act",
    )(idx_blocks, xf, gate, idx, tab)


def _expert_mix_kernel(cw_hbm, tab_ref, o_ref, cw_a, cw_b, sem):
    _with_index_block(cw_hbm, (cw_a, cw_b), sem, partial(_expert_mix_body, tab_ref, o_ref))


def _expert_mix_body(tab_ref, o_ref, cw_ref):
    G = PEER_GROUP
    R = ROWS_PER_EXPERT
    upper = lax.broadcasted_iota(jnp.int32, (SUBLANES, LANES), 0) >= R

    def token(g, carry):
        acc_lo = jnp.zeros((SUBLANES, LANES), jnp.float32)
        acc_hi = jnp.zeros((SUBLANES, LANES), jnp.float32)
        for r in range(0, N_SLOT, 2):
            cw_a = cw_ref.at[pl.ds(r * G, G)][g]
            cw_b = cw_ref.at[pl.ds((r + 1) * G, G)][g]
            lo, hi = _unpack(_pair_rows(tab_ref, cw_a & 0xFFFF, cw_b & 0xFFFF))
            w = pltpu.bitcast(jnp.where(upper, cw_b, cw_a) & jnp.int32(-65536), jnp.float32)
            acc_lo = acc_lo + w * lo
            acc_hi = acc_hi + w * hi
        o_ref[pl.ds(pl.multiple_of(g * 8, 8), 8), :] = jnp.concatenate(
            [acc_lo[0:R] + acc_lo[R:], acc_hi[0:R] + acc_hi[R:]], axis=0)
        return carry

    lax.fori_loop(0, G, token, 0)


def _expert_mix(cw_blocks, tab):
    G = PEER_GROUP
    nb = cw_blocks.shape[0]
    return pl.pallas_call(
        _expert_mix_kernel,
        grid=(nb,),
        in_specs=[pl.BlockSpec(memory_space=pl.ANY),
                  pl.BlockSpec(tab.shape, lambda i: (0, 0), pipeline_mode=pl.Buffered(1))],
        out_specs=pl.BlockSpec((G * 8, LANES), lambda i: (i, 0)),
        out_shape=jax.ShapeDtypeStruct((nb * G * 8, LANES), jnp.float32),
        scratch_shapes=[pltpu.SMEM((N_SLOT * G,), jnp.int32), pltpu.SMEM((N_SLOT * G,), jnp.int32),
                        pltpu.SemaphoreType.DMA((2,))],
        compiler_params=pltpu.CompilerParams(vmem_limit_bytes=TABLE_VMEM_LIMIT),
        name="expert_mix",
    )(cw_blocks, tab)


def _final_kernel(h_ref, pf_ref, g_ref, o_ref):
    p = jnp.concatenate([pf_ref[pl.ds(q, ROW_TILE, stride=FOLD), :] for q in range(FOLD)], axis=1)
    o_ref[...] = _rms(h_ref[...] + p) * g_ref[...]


def _final_norm(h, pf, g):
    n, d = h.shape
    row = pl.BlockSpec((ROW_TILE, d), lambda i: (i, 0))
    return pl.pallas_call(
        _final_kernel,
        grid=(n // ROW_TILE,),
        in_specs=[row, pl.BlockSpec((ROW_TILE * FOLD, LANES), lambda i: (i, 0)), pl.BlockSpec((1, d), lambda i: (0, 0))],
        out_specs=row,
        out_shape=jax.ShapeDtypeStruct((n, d), jnp.float32),
        name="final_norm",
    )(h, pf, g)


def _rope_tables(first_pos, rows):
    half = QK_ROPE // 2
    freqs = ROPE_THETA ** (-jnp.arange(half, dtype=jnp.float32) / half)
    pos = (jnp.arange(rows, dtype=jnp.int32) + first_pos).astype(jnp.float32)
    ang = pos[:, None] * freqs[None, :]
    one = jnp.ones((rows, QK_NOPE), jnp.float32)
    zero = jnp.zeros((rows, HEAD_PAD - QK_NOPE - QK_ROPE), jnp.float32)
    cos = jnp.concatenate([one, jnp.cos(ang), jnp.cos(ang), zero], axis=1)
    sin = jnp.concatenate([0 * one, jnp.sin(ang), jnp.sin(ang), zero], axis=1)
    return cos, sin


def _rope_cols(w):
    half = QK_ROPE // 2
    x1, x2 = w[..., :half], w[..., half:]
    z_front = jnp.zeros(w.shape[:-1] + (QK_NOPE,), w.dtype)
    z_back = jnp.zeros(w.shape[:-1] + (HEAD_PAD - QK_NOPE - QK_ROPE,), w.dtype)
    return (jnp.concatenate([z_front, x1, x2, z_back], axis=-1), jnp.concatenate([z_front, -x2, x1, z_back], axis=-1))


def _prep_weights(w_in, w_uq, w_ukv):
    d = w_in.shape[0]
    c2 = 2 * CONV_CH + Q_LORA + KV_LORA
    kr_a, kr_b = _rope_cols(w_in[:, c2:])
    win = jnp.concatenate([w_in[:, :c2], kr_a, kr_b], axis=1).astype(jnp.bfloat16)
    wq3 = w_uq.reshape(Q_LORA, MLA_HEADS, QK_NOPE + QK_ROPE)
    qa, qb = _rope_cols(wq3[..., QK_NOPE:])
    qa = qa.at[..., :QK_NOPE].set(wq3[..., :QK_NOPE])
    wq = jnp.concatenate([qa.reshape(Q_LORA, -1), qb.reshape(Q_LORA, -1)], axis=1).astype(jnp.bfloat16)
    wkv3 = w_ukv.reshape(KV_LORA, MLA_HEADS, QK_NOPE + V_HEAD)
    zk = jnp.zeros((KV_LORA, MLA_HEADS, HEAD_PAD - QK_NOPE), w_ukv.dtype)
    wk = jnp.concatenate([wkv3[..., :QK_NOPE], zk], axis=-1)
    zv = jnp.zeros((KV_LORA, MLA_HEADS, HEAD_PAD - V_HEAD), w_ukv.dtype)
    v_lo = jnp.concatenate([wkv3[..., QK_NOPE:], zv], axis=-1)
    v_hi = jnp.concatenate([zv, wkv3[..., QK_NOPE:]], axis=-1)
    odd = (jnp.arange(MLA_HEADS) % 2 == 1)[None, :, None]
    wv = jnp.where(odd, v_hi, v_lo)
    wkv = jnp.concatenate([wk.reshape(KV_LORA, -1), wv.reshape(KV_LORA, -1)], axis=1).astype(jnp.bfloat16)
    return win, wq, wkv


def kernel(x, meta, g_mix_norm, w_in, g_q, w_uq, g_kv, w_ukv, conv_w, conv_b, g_conv_ln, b_conv_ln, g_out_attn, g_out_conv, w_out, g_ffn_norm, peer_wq, peer_keys, peer_u, peer_v, g_final):
    B, S, D = x.shape
    assert S % ROW_TILE == 0
    row2 = lambda v: v.reshape(1, -1)
    win, wq, wkv = _prep_weights(w_in[0], w_uq[0], w_ukv[0])
    weights = (row2(g_mix_norm[0]), win, row2(g_q[0]), wq, row2(g_kv[0]), wkv)
    h_meta = jnp.concatenate([jnp.zeros((ZERO_ROWS, D), x.dtype), meta.astype(x.dtype)], axis=0)[None]
    glu_meta, _, k_meta, v_meta = _in_proj(h_meta, weights, *_rope_tables(-ZERO_ROWS, FRONT), FRONT)
    glu, q, k, v = _in_proj(x, weights, *_rope_tables(N_META, S), ROW_TILE)
    convn = _conv_branch(glu_meta, glu, conv_w[0], row2(conv_b[0]), row2(g_conv_ln[0]), row2(b_conv_ln[0]),
                         row2(g_out_conv[0]))
    attn = _attention(q, k_meta, v_meta, k, v)
    n = B * S
    wo = w_out[0].astype(jnp.bfloat16)
    aw = MLA_HEADS * V_HEAD
    h1, xn, xf = _out_proj(x.reshape(n, D), attn.reshape(n, aw), convn.reshape(n, CONV_CH), row2(g_out_attn[0]),
                           wo[:aw], wo[aw:], row2(g_ffn_norm[0]))
    wqk = _fold_keys(peer_keys[0], peer_wq[0])
    idx, gate = _peer_topk(xn, wqk)
    slot_words = _expert_act(_index_blocks(idx), xf, gate, idx, _pack_table(peer_u[0]))
    peer_folded = _expert_mix(_index_blocks(slot_words), _pack_table(peer_v[0]))
    return _final_norm(h1, peer_folded, row2(g_final)).reshape(B, S, D)
```
